```python
import math
import jax, jax.numpy as jnp
from jax import lax
import numpy as np

D_MODEL = 1024
BATCH = 16
SEQ = 2048
DEPTH = 2
DEC_BATCH = 8
DEC_SEQ = 64
PAST_LEN = 1024

CHUNK = 64
N_MIXERS = 2
N_CONV = (DEPTH + 1) // 2
N_ATTN = DEPTH // 2
CONV_EXPAND = 2
CONV_WIDTH = CONV_EXPAND * D_MODEL
CONV_W = 3
N_HEADS = 16
HEAD_DIM = 64
ATT_WIDTH = N_HEADS * HEAD_DIM
Q_BLOCK = 128
RMS_EPS = 1e-6
FORGET_BIAS_INIT = 3.0
NEG_INF = -1e30

kernel_name = "hybrid_shortconv_fox_stream_step"


def rms_norm(x, g):
    x32 = x.astype(jnp.float32)
    y = x32 * lax.rsqrt(jnp.mean(x32 * x32, axis=-1, keepdims=True) + RMS_EPS)
    return (y * g.astype(jnp.float32)).astype(x.dtype)


def ada_norm(x, c, g, w_ada, b_ada):
    mod = jax.nn.silu(c) @ w_ada + b_ada
    shift, scale, gate = jnp.split(mod, 3, axis=-1)
    h = rms_norm(x, g) * (1.0 + scale[:, None, :]) + shift[:, None, :]
    return h, gate


def conv_mixer(h, hist, w_in, conv_k, w_out):
    bg, cg, xv, z = jnp.split(h @ w_in, 4, axis=-1)
    u = cg * xv
    T = u.shape[1]
    full = jnp.concatenate([hist, u], axis=1)
    conv = (conv_k[0] * full[:, 0:T] + conv_k[1] * full[:, 1:T + 1]
            + conv_k[2] * full[:, 2:T + 2])
    y = bg * conv * jax.nn.silu(z)
    return y @ w_out, full[:, -(CONV_W - 1):]


def attn_proj(h, w_in, b_f):
    B, T, _ = h.shape
    proj = h @ w_in
    q, k, v, z = [proj[..., i * ATT_WIDTH:(i + 1) * ATT_WIDTH] for i in range(4)]
    f_logit = proj[..., 4 * ATT_WIDTH:] + b_f
    logf = jax.nn.log_sigmoid(f_logit.astype(jnp.float32)).astype(h.dtype)
    shp = (B, T, N_HEADS, HEAD_DIM)
    return q.reshape(shp), k.reshape(shp), v.reshape(shp), z, logf


def fox_block(q, cq, qpos, k, v, ck, kpos):
    s = jnp.einsum('bqhd,bkhd->bhqk', q.astype(jnp.float32), k.astype(jnp.float32)) / math.sqrt(HEAD_DIM)
    s = s + jnp.transpose(cq, (0, 2, 1))[..., :, None] - jnp.transpose(ck, (0, 2, 1))[..., None, :]
    mask = kpos[None, :] <= qpos[:, None]
    s = jnp.where(mask[None, None], s, jnp.float32(NEG_INF))
    p = jax.nn.softmax(s, axis=-1)
    o = jnp.einsum('bhqk,bkhd->bqhd', p, v.astype(jnp.float32))
    return o.astype(v.dtype)


def fox_prompt(q, k, v, logf):
    B, S = q.shape[0], q.shape[1]
    nb = S // Q_BLOCK
    cum = jnp.cumsum(logf.astype(jnp.float32), axis=1)
    pos = jnp.arange(S)
    qs = q.reshape(B, nb, Q_BLOCK, N_HEADS, HEAD_DIM).swapaxes(0, 1)
    cqs = cum.reshape(B, nb, Q_BLOCK, N_HEADS).swapaxes(0, 1)
    ps = pos.reshape(nb, Q_BLOCK)

    def body(args):
        qb, cqb, pb = args
        return fox_block(qb, cqb, pb, k, v, cum, pos)

    o = lax.map(body, (qs, cqs, ps))
    return o.swapaxes(0, 1).reshape(B, S, N_HEADS, HEAD_DIM)


def fox_sample(q, k_new, v_new, logf_new, cache_k, cache_v, cache_logf):
    P = cache_k.shape[1]
    T = q.shape[1]
    k_all = jnp.concatenate([cache_k, k_new], axis=1)
    v_all = jnp.concatenate([cache_v, v_new], axis=1)
    cum = jnp.cumsum(jnp.concatenate([cache_logf, logf_new], axis=1).astype(jnp.float32), axis=1)
    kpos = jnp.arange(P + T)
    qpos = P + jnp.arange(T)
    return fox_block(q, cum[:, P:], qpos, k_all, v_all, cum, kpos)


def setup_inputs(seed: int = 0) -> dict:
    key = jax.random.key(seed)
    ks = jax.random.split(key, 20)
    f32 = jnp.float32
    nrm = lambda k, shp, s=1.0: (jax.random.normal(k, shp, f32) * s).astype(f32)
    D, E, A, H = D_MODEL, CONV_WIDTH, ATT_WIDTH, N_HEADS
    return {
        "x_prompt": nrm(ks[0], (BATCH, SEQ, D)),
        "x_sample": nrm(ks[1], (DEC_BATCH, DEC_SEQ, D)),
        "c_prompt": nrm(ks[2], (BATCH, D)),
        "c_sample": nrm(ks[3], (DEC_BATCH, D)),
        "state_conv": nrm(ks[4], (N_CONV, DEC_BATCH, CONV_W - 1, E)),
        "cache_k": nrm(ks[5], (N_ATTN, DEC_BATCH, PAST_LEN, H, HEAD_DIM)),
        "cache_v": nrm(ks[6], (N_ATTN, DEC_BATCH, PAST_LEN, H, HEAD_DIM)),
        "cache_logf": jax.nn.log_sigmoid(FORGET_BIAS_INIT + nrm(ks[7], (N_ATTN, DEC_BATCH, PAST_LEN, H))),
        "norm_g": 1.0 + nrm(ks[8], (DEPTH, D), 0.02),
        "ada_w": nrm(ks[9], (DEPTH, D, 3 * D), D ** -0.5),
        "ada_b": nrm(ks[10], (DEPTH, 3 * D), 0.01),
        "conv_w_in": nrm(ks[11], (N_CONV, D, 4 * E), D ** -0.5),
        "conv_k": nrm(ks[12], (N_CONV, CONV_W, E), CONV_W ** -0.5),
        "conv_w_out": nrm(ks[13], (N_CONV, E, D), E ** -0.5),
        "attn_w_in": nrm(ks[14], (N_ATTN, D, 4 * A + H), D ** -0.5),
        "attn_b_f": FORGET_BIAS_INIT + nrm(ks[15], (N_ATTN, H), 0.1),
        "attn_w_out": nrm(ks[16], (N_ATTN, A, D), A ** -0.5),
        "final_g": 1.0 + nrm(ks[17], (D,), 0.02),
    }


def reference(x_prompt, x_sample, c_prompt, c_sample, state_conv, cache_k, cache_v, cache_logf,
              norm_g, ada_w, ada_b, conv_w_in, conv_k, conv_w_out, attn_w_in, attn_b_f, attn_w_out, final_g):
    xp, xs = x_prompt, x_sample
    Bp, Bs = xp.shape[0], xs.shape[0]
    conv_p, conv_s = [], []
    kp_l, vp_l, fp_l, ks_l, vs_l, fs_l = [], [], [], [], [], []
    for i in range(DEPTH):
        j = i // N_MIXERS
        hp, gp = ada_norm(xp, c_prompt, norm_g[i], ada_w[i], ada_b[i])
        hs, gs = ada_norm(xs, c_sample, norm_g[i], ada_w[i], ada_b[i])
        if i % N_MIXERS == 0:
            zero_hist = jnp.zeros((Bp, CONV_W - 1, CONV_WIDTH), hp.dtype)
            op, hist_p = conv_mixer(hp, zero_hist, conv_w_in[j], conv_k[j], conv_w_out[j])
            os_, hist_s = conv_mixer(hs, state_conv[j].astype(hs.dtype), conv_w_in[j], conv_k[j], conv_w_out[j])
            conv_p.append(hist_p)
            conv_s.append(hist_s)
        else:
            qp, kp, vp, zp, lfp = attn_proj(hp, attn_w_in[j], attn_b_f[j])
            ap = fox_prompt(qp, kp, vp, lfp).reshape(Bp, -1, ATT_WIDTH)
            op = (ap * jax.nn.silu(zp)) @ attn_w_out[j]
            qs, ks_, vs_, zs, lfs = attn_proj(hs, attn_w_in[j], attn_b_f[j])
            as_ = fox_sample(qs, ks_, vs_, lfs, cache_k[j], cache_v[j], cache_logf[j]).reshape(Bs, -1, ATT_WIDTH)
            os_ = (as_ * jax.nn.silu(zs)) @ attn_w_out[j]
            kp_l.append(kp); vp_l.append(vp); fp_l.append(lfp)
            ks_l.append(ks_); vs_l.append(vs_); fs_l.append(lfs)
        xp = xp + gp[:, None, :] * op
        xs = xs + gs[:, None, :] * os_
    y_prompt = rms_norm(xp, final_g)
    y_sample = rms_norm(xs, final_g)
    new_conv_prompt = jnp.stack(conv_p)
    new_k_prompt = jnp.stack(kp_l)
    new_v_prompt = jnp.stack(vp_l)
    new_logf_prompt = jnp.stack(fp_l)
    new_conv_sample = jnp.stack(conv_s)
    new_k_sample = jnp.stack(ks_l)
    new_v_sample = jnp.stack(vs_l)
    new_logf_sample = jnp.stack(fs_l)
    return (y_prompt, y_sample, new_conv_prompt, new_k_prompt, new_v_prompt, new_logf_prompt,
            new_conv_sample, new_k_sample, new_v_sample, new_logf_sample)
```

```python
import functools
import math

import jax
import jax.numpy as jnp
from jax import lax
from jax.experimental import pallas as pl
from jax.experimental.pallas import tpu as pltpu

RMS_EPS = 1e-6
NEG_INF = -1e30
CONV_TAPS = 3
HEAD_DIM = 64
LANES = 128
HEADS_PER_BLOCK = LANES // HEAD_DIM
AUG_PARTS = 3
VMEM_LIMIT_BYTES = 56 * 1024 * 1024

_F32 = jnp.float32
_BF16 = jnp.bfloat16


def _const_spec(shape):
    return pl.BlockSpec(shape, lambda *_: (0,) * len(shape), pipeline_mode=pl.Buffered(1))


def _params(*semantics):
    return pltpu.CompilerParams(dimension_semantics=semantics, vmem_limit_bytes=VMEM_LIMIT_BYTES)


def _silu(x):
    return x / (1.0 + jnp.exp(-x))


def _ada_norm(x, mod_ref, g_ref):
    ms = jnp.mean(x * x, axis=-1, keepdims=True)
    y = x * lax.rsqrt(ms + RMS_EPS) * g_ref[...]
    return y * (1.0 + mod_ref[0, 1:2, :]) + mod_ref[0, 0:1, :]


def _mod_kernel(c_ref, w_ref, b_ref, o_ref):
    c = c_ref[...]
    o_ref[0] = jnp.dot(_silu(c), w_ref[0], preferred_element_type=_F32,
                       precision=lax.Precision.HIGHEST) + b_ref[0]


def _modulation(c_all, ada_w, ada_b):
    depth, d, d3 = ada_w.shape
    n = c_all.shape[0]
    col = d
    return pl.pallas_call(
        _mod_kernel,
        grid=(depth, d3 // col),
        in_specs=[
            pl.BlockSpec((n, d), lambda i, j: (0, 0)),
            pl.BlockSpec((1, d, col), lambda i, j: (i, 0, j)),
            pl.BlockSpec((1, 1, col), lambda i, j: (i, 0, j)),
        ],
        out_specs=pl.BlockSpec((1, n, col), lambda i, j: (i, 0, j)),
        out_shape=jax.ShapeDtypeStruct((depth, n, d3), _F32),
        compiler_params=_params("arbitrary", "arbitrary"),
        name="modulation",
    )(c_all, ada_w, ada_b.reshape(depth, 1, d3))


def _conv_kernel(x_ref, mod_ref, g_ref, hist_ref, win_ref, ck_ref, wout_ref,
                 xo_ref, hist_o_ref, carry_ref, *, e_chunk):
    t = pl.program_id(1)
    tm = x_ref.shape[1]
    e_width = ck_ref.shape[1]

    @pl.when(t == 0)
    def _():
        carry_ref[...] = hist_ref[0]

    x = x_ref[0]
    h = _ada_norm(x, mod_ref, g_ref).astype(_BF16)
    row = lax.broadcasted_iota(jnp.int32, (tm, e_chunk), 0)
    acc = jnp.zeros(x.shape, _F32)
    for e0 in range(0, e_width, e_chunk):
        def proj(branch):
            lo = branch * e_width + e0
            return jnp.dot(h, win_ref[:, lo:lo + e_chunk], preferred_element_type=_F32)
        bg, cg, xv, z = proj(0), proj(1), proj(2), proj(3)
        u = cg * xv
        prev2 = carry_ref[0:1, e0:e0 + e_chunk]
        prev1 = carry_ref[1:2, e0:e0 + e_chunk]
        u1 = jnp.where(row == 0, prev1, pltpu.roll(u, 1, 0))
        u2 = jnp.where(row == 0, prev2, jnp.where(row == 1, prev1, pltpu.roll(u, 2, 0)))
        conv = (ck_ref[0:1, e0:e0 + e_chunk] * u2 + ck_ref[1:2, e0:e0 + e_chunk] * u1
                + ck_ref[2:3, e0:e0 + e_chunk] * u)
        y = bg * conv * _silu(z)
        acc = acc + jnp.dot(y.astype(_BF16), wout_ref[e0:e0 + e_chunk, :], preferred_element_type=_F32)
        carry_ref[:, e0:e0 + e_chunk] = u[tm - (CONV_TAPS - 1):, :]
    xo_ref[0] = x + mod_ref[0, 2:3, :] * acc
    hist_o_ref[0] = carry_ref[...]


def _conv_layer(x, mod, g, hist, w_in, conv_k, w_out, *, tm):
    b, s, d = x.shape
    e_width = conv_k.shape[1]
    e_chunk = min(512, e_width)
    assert s % tm == 0 and tm >= CONV_TAPS - 1 and e_width % e_chunk == 0
    return pl.pallas_call(
        functools.partial(_conv_kernel, e_chunk=e_chunk),
        grid=(b, s // tm),
        in_specs=[
            pl.BlockSpec((1, tm, d), lambda i, t: (i, t, 0)),
            pl.BlockSpec((1, 3, d), lambda i, t: (i, 0, 0)),
            _const_spec((1, d)),
            pl.BlockSpec((1, CONV_TAPS - 1, e_width), lambda i, t: (i, 0, 0)),
            _const_spec(w_in.shape),
            _const_spec(conv_k.shape),
            _const_spec(w_out.shape),
        ],
        out_specs=[
            pl.BlockSpec((1, tm, d), lambda i, t: (i, t, 0)),
            pl.BlockSpec((1, CONV_TAPS - 1, e_width), lambda i, t: (i, 0, 0)),
        ],
        out_shape=[
            jax.ShapeDtypeStruct((b, s, d), _F32),
            jax.ShapeDtypeStruct((b, CONV_TAPS - 1, e_width), _F32),
        ],
        scratch_shapes=[pltpu.VMEM((CONV_TAPS - 1, e_width), _F32)],
        compiler_params=_params("arbitrary", "arbitrary"),
        name="conv_layer",
    )(x, mod, g, hist, w_in, conv_k, w_out)


def _attn_proj_kernel(x_ref, mod_ref, g_ref, w_ref, wf_ref, bf_ref,
                      q_ref, k_ref, v_ref, sz_ref, logf_ref, logfx_ref):
    a = q_ref.shape[2]
    n_heads = logf_ref.shape[2]
    h = _ada_norm(x_ref[0], mod_ref, g_ref).astype(_BF16)

    def proj(i):
        return jnp.dot(h, w_ref[:, i * a:(i + 1) * a], preferred_element_type=_F32)

    q_ref[0] = proj(0).astype(_BF16)
    k_ref[0] = proj(1)
    v_ref[0] = proj(2)
    sz_ref[0] = _silu(proj(3)).astype(_BF16)
    f = jnp.dot(h, wf_ref[...], preferred_element_type=_F32) + bf_ref[...]
    logf = jnp.minimum(f, 0.0) - jnp.log1p(jnp.exp(-jnp.abs(f)))
    logfx_ref[0] = logf
    logf_ref[0] = logf[:, :n_heads]


def _attn_proj(x, mod, g, w_qkvz, wf_x, bf_x, *, tm, n_heads):
    b, s, d = x.shape
    a = w_qkvz.shape[1] // 4
    assert s % tm == 0
    tok = lambda width: pl.BlockSpec((1, tm, width), lambda i, t: (i, t, 0))
    return pl.pallas_call(
        _attn_proj_kernel,
        grid=(b, s // tm),
        in_specs=[
            tok(d),
            pl.BlockSpec((1, 3, d), lambda i, t: (i, 0, 0)),
            _const_spec((1, d)),
            _const_spec(w_qkvz.shape),
            _const_spec(wf_x.shape),
            _const_spec(bf_x.shape),
        ],
        out_specs=[tok(a), tok(a), tok(a), tok(a), tok(n_heads), tok(LANES)],
        out_shape=[
            jax.ShapeDtypeStruct((b, s, a), _BF16),
            jax.ShapeDtypeStruct((b, s, a), _F32),
            jax.ShapeDtypeStruct((b, s, a), _F32),
            jax.ShapeDtypeStruct((b, s, a), _BF16),
            jax.ShapeDtypeStruct((b, s, n_heads), _F32),
            jax.ShapeDtypeStruct((b, s, LANES), _F32),
        ],
        compiler_params=_params("arbitrary", "arbitrary"),
        name="attn_proj",
    )(x, mod, g, w_qkvz, wf_x, bf_x)


def _split_bf16(c):
    parts = []
    rest = c
    for _ in range(AUG_PARTS):
        piece = rest.astype(_BF16).astype(_F32)
        parts.append(piece)
        rest = rest - piece
    return parts


def _scan_kernel(logfx_ref, augq_ref, augk_ref, *, n_heads):
    skv = logfx_ref.shape[1]
    sq = augq_ref.shape[1]
    c = logfx_ref[0]
    row = lax.broadcasted_iota(jnp.int32, c.shape, 0)
    shift = 1
    while shift < skv:
        c = c + jnp.where(row >= shift, pltpu.roll(c, shift, 0), 0.0)
        shift *= 2
    parts = _split_bf16(c)
    col = lax.broadcasted_iota(jnp.int32, c.shape, 1) // n_heads
    aq = jnp.zeros_like(c)
    ak = jnp.zeros_like(c)
    for i, piece in enumerate(parts):
        aq = jnp.where(col == i, piece, aq)
        ak = jnp.where(col == AUG_PARTS + i, -piece, ak)
    aq = jnp.where((col >= AUG_PARTS) & (col < 2 * AUG_PARTS), 1.0, aq)
    ak = jnp.where(col < AUG_PARTS, 1.0, ak)
    augk_ref[0] = ak.astype(_BF16)
    augq_ref[0] = aq[skv - sq:, :].astype(_BF16)


def _forget_scan(logfx_all, *, sq, n_heads):
    b, skv, lanes = logfx_all.shape
    assert lanes == LANES and 2 * AUG_PARTS * n_heads <= LANES
    return pl.pallas_call(
        functools.partial(_scan_kernel, n_heads=n_heads),
        grid=(b,),
        in_specs=[pl.BlockSpec((1, skv, LANES), lambda i: (i, 0, 0))],
        out_specs=[
            pl.BlockSpec((1, sq, LANES), lambda i: (i, 0, 0)),
            pl.BlockSpec((1, skv, LANES), lambda i: (i, 0, 0)),
        ],
        out_shape=[
            jax.ShapeDtypeStruct((b, sq, LANES), _BF16),
            jax.ShapeDtypeStruct((b, skv, LANES), _BF16),
        ],
        compiler_params=_params("arbitrary"),
        name="forget_scan",
    )(logfx_all)


def _attention_kernel(q_ref, augq_ref, sz_ref, k_ref, v_ref, augk_ref, a_ref,
                      kk_ref, vv_ref, m_ref, l_ref, acc_ref, *, past, tk, n_heads):
    pair = pl.program_id(1)
    qi = pl.program_id(2)
    tq = q_ref.shape[1]
    skv = k_ref.shape[1]

    @pl.when(qi == 0)
    def _():
        lane = lax.broadcasted_iota(jnp.int32, (skv, LANES), 1)
        kp = k_ref[0].astype(_BF16)
        ak = augk_ref[0]
        zero = jnp.zeros_like(kp)
        for j in range(HEADS_PER_BLOCK):
            head = pair * HEADS_PER_BLOCK + j
            kk_ref[j, :, :LANES] = jnp.where(lane // HEAD_DIM == j, kp, zero)
            kk_ref[j, :, LANES:] = jnp.where(lane % n_heads == head, ak, zero)
        vv_ref[...] = v_ref[0].astype(_BF16)

    q2 = jnp.concatenate([q_ref[0], augq_ref[0]], axis=1)
    q_start = past + qi * tq
    n_full = q_start // tk
    out_lane = lax.broadcasted_iota(jnp.int32, (tq, LANES), 1)
    out = jnp.zeros((tq, LANES), _F32)

    for j in range(HEADS_PER_BLOCK):
        def attend(k_start, k_len, masked):
            ks = kk_ref[j, pl.ds(k_start, k_len), :]
            s = lax.dot_general(q2, ks, (((1,), (1,)), ((), ())), preferred_element_type=_F32)
            if masked:
                r = lax.broadcasted_iota(jnp.int32, s.shape, 0)
                c = lax.broadcasted_iota(jnp.int32, s.shape, 1)
                s = jnp.where(c <= r, s, NEG_INF)
            m_old = m_ref[...]
            m_new = jnp.maximum(m_old, jnp.max(s, axis=-1, keepdims=True))
            alpha = jnp.exp(m_old - m_new)
            p = jnp.exp(s - m_new)
            l_ref[...] = alpha * l_ref[...] + jnp.sum(p, axis=-1, keepdims=True)
            pv = jnp.dot(p.astype(_BF16), vv_ref[pl.ds(k_start, k_len), :], preferred_element_type=_F32)
            acc_ref[...] = alpha * acc_ref[...] + pv
            m_ref[...] = m_new

        m_ref[...] = jnp.full(m_ref.shape, NEG_INF, _F32)
        l_ref[...] = jnp.zeros(l_ref.shape, _F32)
        acc_ref[...] = jnp.zeros(acc_ref.shape, _F32)
        attend(pl.multiple_of(q_start, tq), tq, True)

        def body(kb, carry):
            attend(pl.multiple_of(kb * tk, tk), tk, False)
            return carry
        lax.fori_loop(0, n_full, body, 0)
        o = acc_ref[...] / l_ref[...]
        out = jnp.where(out_lane // HEAD_DIM == j, o, out)
    a_ref[0] = (out * sz_ref[0].astype(_F32)).astype(_BF16)


def _attention(q, augq, sz, k_all, v_all, augk, *, tq, tk, n_heads):
    b, sq, a = q.shape
    skv = k_all.shape[1]
    past = skv - sq
    n_q = sq // tq
    assert sq % tq == 0 and past % tk == 0 and (tq % tk == 0 or n_q == 1) and a == n_heads * HEAD_DIM
    n_pairs = a // LANES
    return pl.pallas_call(
        functools.partial(_attention_kernel, past=past, tk=tk, n_heads=n_heads),
        grid=(b, n_pairs, n_q),
        in_specs=[
            pl.BlockSpec((1, tq, LANES), lambda i, p, t: (i, t, p)),
            pl.BlockSpec((1, tq, LANES), lambda i, p, t: (i, t, 0)),
            pl.BlockSpec((1, tq, LANES), lambda i, p, t: (i, t, p)),
            pl.BlockSpec((1, skv, LANES), lambda i, p, t: (i, 0, p)),
            pl.BlockSpec((1, skv, LANES), lambda i, p, t: (i, 0, p)),
            pl.BlockSpec((1, skv, LANES), lambda i, p, t: (i, 0, 0)),
        ],
        out_specs=pl.BlockSpec((1, tq, LANES), lambda i, p, t: (i, t, p)),
        out_shape=jax.ShapeDtypeStruct((b, sq, a), _BF16),
        scratch_shapes=[
            pltpu.VMEM((HEADS_PER_BLOCK, skv, 2 * LANES), _BF16),
            pltpu.VMEM((skv, LANES), _BF16),
            pltpu.VMEM((tq, 1), _F32),
            pltpu.VMEM((tq, 1), _F32),
            pltpu.VMEM((tq, LANES), _F32),
        ],
        compiler_params=_params("arbitrary", "arbitrary", "arbitrary"),
        name="attention",
    )(q, augq, sz, k_all, v_all, augk)


def _out_kernel(a_ref, x_ref, mod_ref, w_ref, g_ref, y_ref):
    o = jnp.dot(a_ref[0], w_ref[...], preferred_element_type=_F32)
    x = x_ref[0] + mod_ref[0, 2:3, :] * o
    ms = jnp.mean(x * x, axis=-1, keepdims=True)
    y_ref[0] = x * lax.rsqrt(ms + RMS_EPS) * g_ref[...]


def _out_layer(a, x, mod, w_out, final_g, *, tm):
    b, s, d = x.shape
    tok = lambda width: pl.BlockSpec((1, tm, width), lambda i, t: (i, t, 0))
    return pl.pallas_call(
        _out_kernel,
        grid=(b, s // tm),
        in_specs=[
            tok(a.shape[2]),
            tok(d),
            pl.BlockSpec((1, 3, d), lambda i, t: (i, 0, 0)),
            _const_spec(w_out.shape),
            _const_spec((1, d)),
        ],
        out_specs=tok(d),
        out_shape=jax.ShapeDtypeStruct((b, s, d), _F32),
        compiler_params=_params("arbitrary", "arbitrary"),
        name="out_layer",
    )(a, x, mod, w_out, final_g)


def _stream(x, mod0, mod1, hist, past_k, past_v, past_logf, w, *, tm, tq, tk):
    b, s, d = x.shape
    n_heads = w["n_heads"]
    x1, new_hist = _conv_layer(x, mod0, w["g0"], hist, w["conv_w_in"], w["conv_k"], w["conv_w_out"], tm=tm)
    q, k, v, sz, logf, logfx = _attn_proj(x1, mod1, w["g1"], w["attn_w_qkvz"], w["attn_wf_x"], w["attn_bf_x"],
                                          tm=tm, n_heads=n_heads)
    if past_k is None:
        k_all, v_all, logfx_all = k, v, logfx
    else:
        k_all = jnp.concatenate([past_k, k], axis=1)
        v_all = jnp.concatenate([past_v, v], axis=1)
        logfx_all = jnp.concatenate([jnp.tile(past_logf, (1, 1, LANES // n_heads)), logfx], axis=1)
    augq, augk = _forget_scan(logfx_all, sq=s, n_heads=n_heads)
    a = _attention(q, augq, sz, k_all, v_all, augk, tq=tq, tk=tk, n_heads=n_heads)
    y = _out_layer(a, x1, mod1, w["attn_w_out"], w["final_g"], tm=tm)
    return y, new_hist, k, v, logf


def kernel(x_prompt, x_sample, c_prompt, c_sample, state_conv, cache_k, cache_v, cache_logf, norm_g, ada_w, ada_b,
           conv_w_in, conv_k, conv_w_out, attn_w_in, attn_b_f, attn_w_out, final_g):
    bp, sp, d = x_prompt.shape
    bs, ss, _ = x_sample.shape
    n_heads = attn_b_f.shape[1]
    a = n_heads * HEAD_DIM
    past = cache_k.shape[2]
    assert norm_g.shape[0] == 2 and state_conv.shape[0] == 1 and cache_k.shape[0] == 1

    mod = _modulation(jnp.concatenate([c_prompt, c_sample], axis=0), ada_w, ada_b)
    mod = mod.reshape(2, bp + bs, 3, d)

    w_attn = attn_w_in[0]
    q_scale = jnp.concatenate([jnp.full((a,), 1.0 / math.sqrt(HEAD_DIM), _F32), jnp.ones((3 * a,), _F32)])
    w = {
        "n_heads": n_heads,
        "g0": norm_g[0:1],
        "g1": norm_g[1:2],
        "conv_w_in": conv_w_in[0].astype(_BF16),
        "conv_k": conv_k[0],
        "conv_w_out": conv_w_out[0].astype(_BF16),
        "attn_w_qkvz": (w_attn[:, :4 * a] * q_scale).astype(_BF16),
        "attn_wf_x": jnp.tile(w_attn[:, 4 * a:], (1, LANES // n_heads)).astype(_BF16),
        "attn_bf_x": jnp.tile(attn_b_f, (1, LANES // n_heads)),
        "attn_w_out": attn_w_out[0].astype(_BF16),
        "final_g": final_g.reshape(1, d),
    }

    zero_hist = jnp.zeros((bp, CONV_TAPS - 1, conv_k.shape[2]), _F32)
    yp, hist_p, kp, vp, lfp = _stream(x_prompt, mod[0, :bp], mod[1, :bp], zero_hist, None, None, None, w,
                                      tm=512, tq=256, tk=256)
    ys, hist_s, ks, vs, lfs = _stream(x_sample, mod[0, bp:], mod[1, bp:], state_conv[0],
                                      cache_k[0].reshape(bs, past, a), cache_v[0].reshape(bs, past, a),
                                      cache_logf[0], w, tm=ss, tq=ss, tk=past)

    heads = lambda t: t.reshape(1, t.shape[0], t.shape[1], n_heads, HEAD_DIM)
    return (yp, ys, hist_p[None], heads(kp), heads(vp), lfp[None],
            hist_s[None], heads(ks), heads(vs), lfs[None])
```

```python
import functools
import math

import jax
import jax.numpy as jnp
from jax import lax
from jax.experimental import pallas as pl
from jax.experimental.pallas import tpu as pltpu

RMS_EPS = 1e-6
NEG_INF = -1e30
CONV_TAPS = 3
HEAD_DIM = 64
LANES = 128
HEADS_PER_BLOCK = LANES // HEAD_DIM
AUG_PARTS = 3
DIAG_BAND = 256
LOG2_E = math.log2(math.e)
VMEM_LIMIT_BYTES = 56 * 1024 * 1024

_F32 = jnp.float32
_BF16 = jnp.bfloat16


def _const_spec(shape):
    return pl.BlockSpec(shape, lambda *_: (0,) * len(shape), pipeline_mode=pl.Buffered(1))


def _params(*semantics):
    return pltpu.CompilerParams(dimension_semantics=semantics, vmem_limit_bytes=VMEM_LIMIT_BYTES)


def _silu(x):
    return x / (1.0 + jnp.exp(-x))


def _ada_norm(x, mod_ref, g_ref):
    ms = jnp.mean(x * x, axis=-1, keepdims=True)
    y = x * lax.rsqrt(ms + RMS_EPS) * g_ref[...]
    return y * (1.0 + mod_ref[0, 1:2, :]) + mod_ref[0, 0:1, :]


def _mod_kernel(c_ref, w_ref, b_ref, o_ref):
    c = c_ref[...]
    o_ref[0] = jnp.dot(_silu(c), w_ref[0], preferred_element_type=_F32,
                       precision=lax.Precision.HIGHEST) + b_ref[0]


def _modulation(c_all, ada_w, ada_b):
    depth, d, d3 = ada_w.shape
    n = c_all.shape[0]
    col = d
    return pl.pallas_call(
        _mod_kernel,
        grid=(depth, d3 // col),
        in_specs=[
            pl.BlockSpec((n, d), lambda i, j: (0, 0)),
            pl.BlockSpec((1, d, col), lambda i, j: (i, 0, j)),
            pl.BlockSpec((1, 1, col), lambda i, j: (i, 0, j)),
        ],
        out_specs=pl.BlockSpec((1, n, col), lambda i, j: (i, 0, j)),
        out_shape=jax.ShapeDtypeStruct((depth, n, d3), _F32),
        compiler_params=_params("arbitrary", "arbitrary"),
        name="modulation",
    )(c_all, ada_w, ada_b.reshape(depth, 1, d3))


def _conv_kernel(x_ref, mod_ref, g_ref, hist_ref, win_ref, ck_ref, wout_ref,
                 xo_ref, hist_o_ref, carry_ref, *, e_chunk):
    t = pl.program_id(1)
    tm = x_ref.shape[1]
    e_width = ck_ref.shape[1]

    @pl.when(t == 0)
    def _():
        carry_ref[...] = hist_ref[0]

    x = x_ref[0]
    h = _ada_norm(x, mod_ref, g_ref).astype(_BF16)
    row = lax.broadcasted_iota(jnp.int32, (tm, e_chunk), 0)
    acc = jnp.zeros(x.shape, _F32)
    for e0 in range(0, e_width, e_chunk):
        def proj(branch):
            lo = branch * e_width + e0
            return jnp.dot(h, win_ref[:, lo:lo + e_chunk], preferred_element_type=_F32)
        bg, cg, xv, z = proj(0), proj(1), proj(2), proj(3)
        u = cg * xv
        prev2 = carry_ref[0:1, e0:e0 + e_chunk]
        prev1 = carry_ref[1:2, e0:e0 + e_chunk]
        u1 = jnp.where(row == 0, prev1, pltpu.roll(u, 1, 0))
        u2 = jnp.where(row == 0, prev2, jnp.where(row == 1, prev1, pltpu.roll(u, 2, 0)))
        conv = (ck_ref[0:1, e0:e0 + e_chunk] * u2 + ck_ref[1:2, e0:e0 + e_chunk] * u1
                + ck_ref[2:3, e0:e0 + e_chunk] * u)
        y = bg * conv * _silu(z)
        acc = acc + jnp.dot(y.astype(_BF16), wout_ref[e0:e0 + e_chunk, :], preferred_element_type=_F32)
        carry_ref[:, e0:e0 + e_chunk] = u[tm - (CONV_TAPS - 1):, :]
    xo_ref[0] = x + mod_ref[0, 2:3, :] * acc
    hist_o_ref[0] = carry_ref[...]


def _conv_layer(x, mod, g, hist, w_in, conv_k, w_out, *, tm):
    b, s, d = x.shape
    e_width = conv_k.shape[1]
    e_chunk = min(512, e_width)
    assert s % tm == 0 and tm >= CONV_TAPS - 1 and e_width % e_chunk == 0
    return pl.pallas_call(
        functools.partial(_conv_kernel, e_chunk=e_chunk),
        grid=(b, s // tm),
        in_specs=[
            pl.BlockSpec((1, tm, d), lambda i, t: (i, t, 0)),
            pl.BlockSpec((1, 3, d), lambda i, t: (i, 0, 0)),
            _const_spec((1, d)),
            pl.BlockSpec((1, CONV_TAPS - 1, e_width), lambda i, t: (i, 0, 0)),
            _const_spec(w_in.shape),
            _const_spec(conv_k.shape),
            _const_spec(w_out.shape),
        ],
        out_specs=[
            pl.BlockSpec((1, tm, d), lambda i, t: (i, t, 0)),
            pl.BlockSpec((1, CONV_TAPS - 1, e_width), lambda i, t: (i, 0, 0)),
        ],
        out_shape=[
            jax.ShapeDtypeStruct((b, s, d), _F32),
            jax.ShapeDtypeStruct((b, CONV_TAPS - 1, e_width), _F32),
        ],
        scratch_shapes=[pltpu.VMEM((CONV_TAPS - 1, e_width), _F32)],
        compiler_params=_params("arbitrary", "arbitrary"),
        name="conv_layer",
    )(x, mod, g, hist, w_in, conv_k, w_out)


def _attn_proj_kernel(x_ref, mod_ref, g_ref, w_ref, wf_ref, bf_ref,
                      q_ref, k_ref, v_ref, sz_ref, logf_ref, logfx_ref):
    a = q_ref.shape[2]
    n_heads = logf_ref.shape[2]
    h = _ada_norm(x_ref[0], mod_ref, g_ref).astype(_BF16)

    def proj(i):
        return jnp.dot(h, w_ref[:, i * a:(i + 1) * a], preferred_element_type=_F32)

    q_ref[0] = proj(0).astype(_BF16)
    k_ref[0] = proj(1)
    v_ref[0] = proj(2)
    sz_ref[0] = _silu(proj(3)).astype(_BF16)
    f = jnp.dot(h, wf_ref[...], preferred_element_type=_F32) + bf_ref[...]
    logf = jnp.minimum(f, 0.0) - jnp.log1p(jnp.exp(-jnp.abs(f)))
    logfx_ref[0] = logf
    logf_ref[0] = logf[:, :n_heads]


def _attn_proj(x, mod, g, w_qkvz, wf_x, bf_x, *, tm, n_heads):
    b, s, d = x.shape
    a = w_qkvz.shape[1] // 4
    assert s % tm == 0
    tok = lambda width: pl.BlockSpec((1, tm, width), lambda i, t: (i, t, 0))
    return pl.pallas_call(
        _attn_proj_kernel,
        grid=(b, s // tm),
        in_specs=[
            tok(d),
            pl.BlockSpec((1, 3, d), lambda i, t: (i, 0, 0)),
            _const_spec((1, d)),
            _const_spec(w_qkvz.shape),
            _const_spec(wf_x.shape),
            _const_spec(bf_x.shape),
        ],
        out_specs=[tok(a), tok(a), tok(a), tok(a), tok(n_heads), tok(LANES)],
        out_shape=[
            jax.ShapeDtypeStruct((b, s, a), _BF16),
            jax.ShapeDtypeStruct((b, s, a), _F32),
            jax.ShapeDtypeStruct((b, s, a), _F32),
            jax.ShapeDtypeStruct((b, s, a), _BF16),
            jax.ShapeDtypeStruct((b, s, n_heads), _F32),
            jax.ShapeDtypeStruct((b, s, LANES), _F32),
        ],
        compiler_params=_params("arbitrary", "arbitrary"),
        name="attn_proj",
    )(x, mod, g, w_qkvz, wf_x, bf_x)


def _split_bf16(c):
    parts = []
    rest = c
    for _ in range(AUG_PARTS):
        piece = rest.astype(_BF16).astype(_F32)
        parts.append(piece)
        rest = rest - piece
    return parts


def _scan_kernel(logfx_ref, augq_ref, augk_ref, *, n_heads):
    skv = logfx_ref.shape[1]
    sq = augq_ref.shape[1]
    c = logfx_ref[0]
    row = lax.broadcasted_iota(jnp.int32, c.shape, 0)
    shift = 1
    while shift < skv:
        c = c + jnp.where(row >= shift, pltpu.roll(c, shift, 0), 0.0)
        shift *= 2
    parts = _split_bf16(c * LOG2_E)
    col = lax.broadcasted_iota(jnp.int32, c.shape, 1) // n_heads
    aq = jnp.zeros_like(c)
    ak = jnp.zeros_like(c)
    for i, piece in enumerate(parts):
        aq = jnp.where(col == i, piece, aq)
        ak = jnp.where(col == AUG_PARTS + i, -piece, ak)
    aq = jnp.where((col >= AUG_PARTS) & (col < 2 * AUG_PARTS), 1.0, aq)
    ak = jnp.where(col < AUG_PARTS, 1.0, ak)
    augk_ref[0] = ak.astype(_BF16)
    augq_ref[0] = aq[skv - sq:, :].astype(_BF16)


def _forget_scan(logfx_all, *, sq, n_heads):
    b, skv, lanes = logfx_all.shape
    assert lanes == LANES and 2 * AUG_PARTS * n_heads <= LANES
    return pl.pallas_call(
        functools.partial(_scan_kernel, n_heads=n_heads),
        grid=(b,),
        in_specs=[pl.BlockSpec((1, skv, LANES), lambda i: (i, 0, 0))],
        out_specs=[
            pl.BlockSpec((1, sq, LANES), lambda i: (i, 0, 0)),
            pl.BlockSpec((1, skv, LANES), lambda i: (i, 0, 0)),
        ],
        out_shape=[
            jax.ShapeDtypeStruct((b, sq, LANES), _BF16),
            jax.ShapeDtypeStruct((b, skv, LANES), _BF16),
        ],
        compiler_params=_params("arbitrary"),
        name="forget_scan",
    )(logfx_all)


def _attention_kernel(q_ref, augq_ref, sz_ref, k_ref, v_ref, augk_ref, a_ref,
                      kk_ref, vv_ref, s_ref, m_ref, l_ref, acc_ref, *, past, tk, n_heads):
    pair = pl.program_id(1)
    qi = pl.program_id(2)
    tq = q_ref.shape[1]
    skv = k_ref.shape[1]

    @pl.when(qi == 0)
    def _():
        lane = lax.broadcasted_iota(jnp.int32, (skv, LANES), 1)
        kp = k_ref[0].astype(_BF16)
        ak = augk_ref[0]
        zero = jnp.zeros_like(kp)
        for j in range(HEADS_PER_BLOCK):
            head = pair * HEADS_PER_BLOCK + j
            kk_ref[j, :, :LANES] = jnp.where(lane // HEAD_DIM == j, kp, zero)
            kk_ref[j, :, LANES:] = jnp.where(lane % n_heads == head, ak, zero)
        vv_ref[...] = v_ref[0].astype(_BF16)

    q2 = jnp.concatenate([q_ref[0], augq_ref[0]], axis=1)
    q_start = past + qi * tq
    n_full = q_start // tk
    heads = range(HEADS_PER_BLOCK)

    def lane_groups(width):
        return [(g, min(LANES, width - g)) for g in range(0, width, LANES)]

    band = DIAG_BAND if tq % DIAG_BAND == 0 else tq
    diag_start = pl.multiple_of(q_start, tq)
    bands = [(r0, r0 + band) for r0 in range(0, tq, band)]

    def scores(c, k_start, row0, rows, width, diag):
        for j in heads:
            ks = kk_ref[j, pl.ds(k_start, width), :]
            s = lax.dot_general(q2[row0:row0 + rows], ks, (((1,), (1,)), ((), ())), preferred_element_type=_F32)
            if diag:
                tail = s[:, width - rows:]
                r = lax.broadcasted_iota(jnp.int32, tail.shape, 0)
                col = lax.broadcasted_iota(jnp.int32, tail.shape, 1)
                tail = jnp.where(col <= r, tail, NEG_INF)
                s = tail if width == rows else jnp.concatenate([s[:, :width - rows], tail], axis=1)
            s_ref[j, c, row0:row0 + rows, :width] = s
            for g, gw in lane_groups(width):
                m_ref[j, row0:row0 + rows, :gw] = jnp.maximum(m_ref[j, row0:row0 + rows, :gw], s[:, g:g + gw])

    m_ref[...] = jnp.full(m_ref.shape, NEG_INF, _F32)

    def pass1(c, carry):
        scores(c, pl.multiple_of(c * tk, tk), 0, tq, tk, False)
        return carry
    lax.fori_loop(0, n_full, pass1, 0)
    for row0, width in bands:
        scores(n_full, diag_start, row0, band, width, True)

    for j in heads:
        row_max = jnp.max(m_ref[j], axis=-1, keepdims=True)
        m_ref[j] = jnp.broadcast_to(row_max, (tq, LANES))
    l_ref[...] = jnp.zeros(l_ref.shape, _F32)
    acc_ref[...] = jnp.zeros(acc_ref.shape, _F32)

    def weights(c, k_start, row0, rows, width):
        for j in heads:
            pieces = []
            for g, gw in lane_groups(width):
                p = jnp.exp2(s_ref[j, c, row0:row0 + rows, g:g + gw] - m_ref[j, row0:row0 + rows, :gw])
                l_ref[j, row0:row0 + rows, :gw] += p
                pieces.append(p.astype(_BF16))
            p_all = pieces[0] if len(pieces) == 1 else jnp.concatenate(pieces, axis=1)
            acc_ref[j, row0:row0 + rows, :] += jnp.dot(p_all, vv_ref[pl.ds(k_start, width), :],
                                                       preferred_element_type=_F32)

    def pass2(c, carry):
        weights(c, pl.multiple_of(c * tk, tk), 0, tq, tk)
        return carry
    lax.fori_loop(0, n_full, pass2, 0)
    for row0, width in bands:
        weights(n_full, diag_start, row0, band, width)

    out_lane = lax.broadcasted_iota(jnp.int32, (tq, LANES), 1)
    out = jnp.zeros((tq, LANES), _F32)
    for j in heads:
        o = acc_ref[j] / jnp.sum(l_ref[j], axis=-1, keepdims=True)
        out = jnp.where(out_lane // HEAD_DIM == j, o, out)
    a_ref[0] = (out * sz_ref[0].astype(_F32)).astype(_BF16)


def _attention(q, augq, sz, k_all, v_all, augk, *, tq, tk, n_heads):
    b, sq, a = q.shape
    skv = k_all.shape[1]
    past = skv - sq
    n_q = sq // tq
    assert sq % tq == 0 and past % tk == 0 and (tq % tk == 0 or n_q == 1) and a == n_heads * HEAD_DIM
    n_pairs = a // LANES
    n_chunks = (past + (n_q - 1) * tq) // tk + 1
    return pl.pallas_call(
        functools.partial(_attention_kernel, past=past, tk=tk, n_heads=n_heads),
        grid=(b, n_pairs, n_q),
        in_specs=[
            pl.BlockSpec((1, tq, LANES), lambda i, p, t: (i, t, p)),
            pl.BlockSpec((1, tq, LANES), lambda i, p, t: (i, t, 0)),
            pl.BlockSpec((1, tq, LANES), lambda i, p, t: (i, t, p)),
            pl.BlockSpec((1, skv, LANES), lambda i, p, t: (i, 0, p)),
            pl.BlockSpec((1, skv, LANES), lambda i, p, t: (i, 0, p)),
            pl.BlockSpec((1, skv, LANES), lambda i, p, t: (i, 0, 0)),
        ],
        out_specs=pl.BlockSpec((1, tq, LANES), lambda i, p, t: (i, t, p)),
        out_shape=jax.ShapeDtypeStruct((b, sq, a), _BF16),
        scratch_shapes=[
            pltpu.VMEM((HEADS_PER_BLOCK, skv, 2 * LANES), _BF16),
            pltpu.VMEM((skv, LANES), _BF16),
            pltpu.VMEM((HEADS_PER_BLOCK, n_chunks, tq, max(tk, tq)), _F32),
            pltpu.VMEM((HEADS_PER_BLOCK, tq, LANES), _F32),
            pltpu.VMEM((HEADS_PER_BLOCK, tq, LANES), _F32),
            pltpu.VMEM((HEADS_PER_BLOCK, tq, LANES), _F32),
        ],
        compiler_params=_params("arbitrary", "arbitrary", "arbitrary"),
        name="attention",
    )(q, augq, sz, k_all, v_all, augk)


def _out_kernel(a_ref, x_ref, mod_ref, w_ref, g_ref, y_ref):
    o = jnp.dot(a_ref[0], w_ref[...], preferred_element_type=_F32)
    x = x_ref[0] + mod_ref[0, 2:3, :] * o
    ms = jnp.mean(x * x, axis=-1, keepdims=True)
    y_ref[0] = x * lax.rsqrt(ms + RMS_EPS) * g_ref[...]


def _out_layer(a, x, mod, w_out, final_g, *, tm):
    b, s, d = x.shape
    tok = lambda width: pl.BlockSpec((1, tm, width), lambda i, t: (i, t, 0))
    return pl.pallas_call(
        _out_kernel,
        grid=(b, s // tm),
        in_specs=[
            tok(a.shape[2]),
            tok(d),
            pl.BlockSpec((1, 3, d), lambda i, t: (i, 0, 0)),
            _const_spec(w_out.shape),
            _const_spec((1, d)),
        ],
        out_specs=tok(d),
        out_shape=jax.ShapeDtypeStruct((b, s, d), _F32),
        compiler_params=_params("arbitrary", "arbitrary"),
        name="out_layer",
    )(a, x, mod, w_out, final_g)


def _stream(x, mod0, mod1, hist, past_k, past_v, past_logf, w, *, tm, tq, tk):
    b, s, d = x.shape
    n_heads = w["n_heads"]
    x1, new_hist = _conv_layer(x, mod0, w["g0"], hist, w["conv_w_in"], w["conv_k"], w["conv_w_out"], tm=tm)
    q, k, v, sz, logf, logfx = _attn_proj(x1, mod1, w["g1"], w["attn_w_qkvz"], w["attn_wf_x"], w["attn_bf_x"],
                                          tm=tm, n_heads=n_heads)
    if past_k is None:
        k_all, v_all, logfx_all = k, v, logfx
    else:
        k_all = jnp.concatenate([past_k, k], axis=1)
        v_all = jnp.concatenate([past_v, v], axis=1)
        logfx_all = jnp.concatenate([jnp.tile(past_logf, (1, 1, LANES // n_heads)), logfx], axis=1)
    augq, augk = _forget_scan(logfx_all, sq=s, n_heads=n_heads)
    a = _attention(q, augq, sz, k_all, v_all, augk, tq=tq, tk=tk, n_heads=n_heads)
    y = _out_layer(a, x1, mod1, w["attn_w_out"], w["final_g"], tm=tm)
    return y, new_hist, k, v, logf


def kernel(x_prompt, x_sample, c_prompt, c_sample, state_conv, cache_k, cache_v, cache_logf, norm_g, ada_w, ada_b,
           conv_w_in, conv_k, conv_w_out, attn_w_in, attn_b_f, attn_w_out, final_g):
    bp, sp, d = x_prompt.shape
    bs, ss, _ = x_sample.shape
    n_heads = attn_b_f.shape[1]
    a = n_heads * HEAD_DIM
    past = cache_k.shape[2]
    assert norm_g.shape[0] == 2 and state_conv.shape[0] == 1 and cache_k.shape[0] == 1

    mod = _modulation(jnp.concatenate([c_prompt, c_sample], axis=0), ada_w, ada_b)
    mod = mod.reshape(2, bp + bs, 3, d)

    w_attn = attn_w_in[0]
    q_scale = jnp.concatenate([jnp.full((a,), LOG2_E / math.sqrt(HEAD_DIM), _F32), jnp.ones((3 * a,), _F32)])
    w = {
        "n_heads": n_heads,
        "g0": norm_g[0:1],
        "g1": norm_g[1:2],
        "conv_w_in": conv_w_in[0].astype(_BF16),
        "conv_k": conv_k[0],
        "conv_w_out": conv_w_out[0].astype(_BF16),
        "attn_w_qkvz": (w_attn[:, :4 * a] * q_scale).astype(_BF16),
        "attn_wf_x": jnp.tile(w_attn[:, 4 * a:], (1, LANES // n_heads)).astype(_BF16),
        "attn_bf_x": jnp.tile(attn_b_f, (1, LANES // n_heads)),
        "attn_w_out": attn_w_out[0].astype(_BF16),
        "final_g": final_g.reshape(1, d),
    }

    zero_hist = jnp.zeros((bp, CONV_TAPS - 1, conv_k.shape[2]), _F32)
    yp, hist_p, kp, vp, lfp = _stream(x_prompt, mod[0, :bp], mod[1, :bp], zero_hist, None, None, None, w,
                                      tm=512, tq=512, tk=512)
    ys, hist_s, ks, vs, lfs = _stream(x_sample, mod[0, bp:], mod[1, bp:], state_conv[0],
                                      cache_k[0].reshape(bs, past, a), cache_v[0].reshape(bs, past, a),
                                      cache_logf[0], w, tm=ss, tq=ss, tk=past)

    heads = lambda t: t.reshape(1, t.shape[0], t.shape[1], n_heads, HEAD_DIM)
    return (yp, ys, hist_p[None], heads(kp), heads(vp), lfp[None],
            hist_s[None], heads(ks), heads(vs), lfs[None])
```

```python
import functools
import math

import jax
import jax.numpy as jnp
from jax import lax
from jax.experimental import pallas as pl
from jax.experimental.pallas import tpu as pltpu

RMS_EPS = 1e-6
NEG_INF = -1e30
CONV_TAPS = 3
HEAD_DIM = 64
LANES = 128
HEADS_PER_BLOCK = LANES // HEAD_DIM
AUG_PARTS = 3
DIAG_BAND = 256
LOG2_E = math.log2(math.e)
ATTN_PAIRS_PER_STEP = 2
VMEM_LIMIT_BYTES = 56 * 1024 * 1024

_F32 = jnp.float32
_BF16 = jnp.bfloat16


def _const_spec(shape):
    return pl.BlockSpec(shape, lambda *_: (0,) * len(shape), pipeline_mode=pl.Buffered(1))


def _params(*semantics):
    return pltpu.CompilerParams(dimension_semantics=semantics, vmem_limit_bytes=VMEM_LIMIT_BYTES)


def _silu(x):
    return x / (1.0 + jnp.exp(-x))


def _ada_norm(x, mod_ref, g_ref):
    ms = jnp.mean(x * x, axis=-1, keepdims=True)
    y = x * lax.rsqrt(ms + RMS_EPS) * g_ref[...]
    return y * (1.0 + mod_ref[0, 1:2, :]) + mod_ref[0, 0:1, :]


def _mod_kernel(c_ref, w_ref, b_ref, o_ref):
    c = c_ref[...]
    o_ref[0] = jnp.dot(_silu(c), w_ref[0], preferred_element_type=_F32,
                       precision=lax.Precision.HIGHEST) + b_ref[0]


def _modulation(c_all, ada_w, ada_b):
    depth, d, d3 = ada_w.shape
    n = c_all.shape[0]
    col = d
    return pl.pallas_call(
        _mod_kernel,
        grid=(depth, d3 // col),
        in_specs=[
            pl.BlockSpec((n, d), lambda i, j: (0, 0)),
            pl.BlockSpec((1, d, col), lambda i, j: (i, 0, j)),
            pl.BlockSpec((1, 1, col), lambda i, j: (i, 0, j)),
        ],
        out_specs=pl.BlockSpec((1, n, col), lambda i, j: (i, 0, j)),
        out_shape=jax.ShapeDtypeStruct((depth, n, d3), _F32),
        compiler_params=_params("arbitrary", "arbitrary"),
        name="modulation",
    )(c_all, ada_w, ada_b.reshape(depth, 1, d3))


def _conv_kernel(x_ref, mod_ref, g_ref, hist_ref, win_ref, ck_ref, wout_ref,
                 xo_ref, hist_o_ref, carry_ref, *, e_chunk):
    t = pl.program_id(1)
    tm = x_ref.shape[1]
    e_width = ck_ref.shape[1]

    @pl.when(t == 0)
    def _():
        carry_ref[...] = hist_ref[0]

    x = x_ref[0]
    h = _ada_norm(x, mod_ref, g_ref).astype(_BF16)
    row = lax.broadcasted_iota(jnp.int32, (tm, e_chunk), 0)
    acc = jnp.zeros(x.shape, _F32)
    for e0 in range(0, e_width, e_chunk):
        def proj(branch):
            lo = branch * e_width + e0
            return jnp.dot(h, win_ref[:, lo:lo + e_chunk], preferred_element_type=_F32)
        bg, cg, xv, z = proj(0), proj(1), proj(2), proj(3)
        u = cg * xv
        prev2 = carry_ref[0:1, e0:e0 + e_chunk]
        prev1 = carry_ref[1:2, e0:e0 + e_chunk]
        u1 = jnp.where(row == 0, prev1, pltpu.roll(u, 1, 0))
        u2 = jnp.where(row == 0, prev2, jnp.where(row == 1, prev1, pltpu.roll(u, 2, 0)))
        conv = (ck_ref[0:1, e0:e0 + e_chunk] * u2 + ck_ref[1:2, e0:e0 + e_chunk] * u1
                + ck_ref[2:3, e0:e0 + e_chunk] * u)
        y = bg * conv * _silu(z)
        acc = acc + jnp.dot(y.astype(_BF16), wout_ref[e0:e0 + e_chunk, :], preferred_element_type=_F32)
        carry_ref[:, e0:e0 + e_chunk] = u[tm - (CONV_TAPS - 1):, :]
    xo_ref[0] = x + mod_ref[0, 2:3, :] * acc
    hist_o_ref[0] = carry_ref[...]


def _conv_layer(x, mod, g, hist, w_in, conv_k, w_out, *, tm):
    b, s, d = x.shape
    e_width = conv_k.shape[1]
    e_chunk = min(512, e_width)
    assert s % tm == 0 and tm >= CONV_TAPS - 1 and e_width % e_chunk == 0
    return pl.pallas_call(
        functools.partial(_conv_kernel, e_chunk=e_chunk),
        grid=(b, s // tm),
        in_specs=[
            pl.BlockSpec((1, tm, d), lambda i, t: (i, t, 0)),
            pl.BlockSpec((1, 3, d), lambda i, t: (i, 0, 0)),
            _const_spec((1, d)),
            pl.BlockSpec((1, CONV_TAPS - 1, e_width), lambda i, t: (i, 0, 0)),
            _const_spec(w_in.shape),
            _const_spec(conv_k.shape),
            _const_spec(w_out.shape),
        ],
        out_specs=[
            pl.BlockSpec((1, tm, d), lambda i, t: (i, t, 0)),
            pl.BlockSpec((1, CONV_TAPS - 1, e_width), lambda i, t: (i, 0, 0)),
        ],
        out_shape=[
            jax.ShapeDtypeStruct((b, s, d), _F32),
            jax.ShapeDtypeStruct((b, CONV_TAPS - 1, e_width), _F32),
        ],
        scratch_shapes=[pltpu.VMEM((CONV_TAPS - 1, e_width), _F32)],
        compiler_params=_params("arbitrary", "arbitrary"),
        name="conv_layer",
    )(x, mod, g, hist, w_in, conv_k, w_out)


def _attn_proj_kernel(x_ref, mod_ref, g_ref, w_ref, wf_ref, bf_ref,
                      q_ref, k_ref, v_ref, sz_ref, logf_ref, logfx_ref):
    a = q_ref.shape[2]
    n_heads = logf_ref.shape[2]
    h = _ada_norm(x_ref[0], mod_ref, g_ref).astype(_BF16)

    def proj(i):
        return jnp.dot(h, w_ref[:, i * a:(i + 1) * a], preferred_element_type=_F32)

    q_ref[0] = proj(0).astype(_BF16)
    k_ref[0] = proj(1)
    v_ref[0] = proj(2)
    sz_ref[0] = _silu(proj(3)).astype(_BF16)
    f = jnp.dot(h, wf_ref[...], preferred_element_type=_F32) + bf_ref[...]
    logf = jnp.minimum(f, 0.0) - jnp.log1p(jnp.exp(-jnp.abs(f)))
    logfx_ref[0] = logf
    logf_ref[0] = logf[:, :n_heads]


def _attn_proj(x, mod, g, w_qkvz, wf_x, bf_x, *, tm, n_heads):
    b, s, d = x.shape
    a = w_qkvz.shape[1] // 4
    assert s % tm == 0
    tok = lambda width: pl.BlockSpec((1, tm, width), lambda i, t: (i, t, 0))
    return pl.pallas_call(
        _attn_proj_kernel,
        grid=(b, s // tm),
        in_specs=[
            tok(d),
            pl.BlockSpec((1, 3, d), lambda i, t: (i, 0, 0)),
            _const_spec((1, d)),
            _const_spec(w_qkvz.shape),
            _const_spec(wf_x.shape),
            _const_spec(bf_x.shape),
        ],
        out_specs=[tok(a), tok(a), tok(a), tok(a), tok(n_heads), tok(LANES)],
        out_shape=[
            jax.ShapeDtypeStruct((b, s, a), _BF16),
            jax.ShapeDtypeStruct((b, s, a), _F32),
            jax.ShapeDtypeStruct((b, s, a), _F32),
            jax.ShapeDtypeStruct((b, s, a), _BF16),
            jax.ShapeDtypeStruct((b, s, n_heads), _F32),
            jax.ShapeDtypeStruct((b, s, LANES), _F32),
        ],
        compiler_params=_params("arbitrary", "arbitrary"),
        name="attn_proj",
    )(x, mod, g, w_qkvz, wf_x, bf_x)


def _split_bf16(c):
    parts = []
    rest = c
    for _ in range(AUG_PARTS):
        piece = rest.astype(_BF16).astype(_F32)
        parts.append(piece)
        rest = rest - piece
    return parts


def _scan_kernel(logfx_ref, augq_ref, augk_ref, *, n_heads):
    skv = logfx_ref.shape[1]
    sq = augq_ref.shape[1]
    c = logfx_ref[0]
    row = lax.broadcasted_iota(jnp.int32, c.shape, 0)
    shift = 1
    while shift < skv:
        c = c + jnp.where(row >= shift, pltpu.roll(c, shift, 0), 0.0)
        shift *= 2
    parts = _split_bf16(c * LOG2_E)
    col = lax.broadcasted_iota(jnp.int32, c.shape, 1) // n_heads
    aq = jnp.zeros_like(c)
    ak = jnp.zeros_like(c)
    for i, piece in enumerate(parts):
        aq = jnp.where(col == i, piece, aq)
        ak = jnp.where(col == AUG_PARTS + i, -piece, ak)
    aq = jnp.where((col >= AUG_PARTS) & (col < 2 * AUG_PARTS), 1.0, aq)
    ak = jnp.where(col < AUG_PARTS, 1.0, ak)
    augk_ref[0] = ak.astype(_BF16)
    augq_ref[0] = aq[skv - sq:, :].astype(_BF16)


def _forget_scan(logfx_all, *, sq, n_heads):
    b, skv, lanes = logfx_all.shape
    assert lanes == LANES and 2 * AUG_PARTS * n_heads <= LANES
    return pl.pallas_call(
        functools.partial(_scan_kernel, n_heads=n_heads),
        grid=(b,),
        in_specs=[pl.BlockSpec((1, skv, LANES), lambda i: (i, 0, 0))],
        out_specs=[
            pl.BlockSpec((1, sq, LANES), lambda i: (i, 0, 0)),
            pl.BlockSpec((1, skv, LANES), lambda i: (i, 0, 0)),
        ],
        out_shape=[
            jax.ShapeDtypeStruct((b, sq, LANES), _BF16),
            jax.ShapeDtypeStruct((b, skv, LANES), _BF16),
        ],
        compiler_params=_params("arbitrary"),
        name="forget_scan",
    )(logfx_all)


def _attention_kernel(q_ref, augq_ref, sz_ref, k_ref, v_ref, augk_ref, a_ref,
                      kk_ref, vv_ref, s_ref, m_ref, l_ref, acc_ref, *, past, tk, n_heads, pairs):
    step = pl.program_id(1)
    qi = pl.program_id(2)
    tq = q_ref.shape[1]
    skv = k_ref.shape[1]
    heads = [(g, g * HEADS_PER_BLOCK + j) for g in range(pairs) for j in range(HEADS_PER_BLOCK)]

    @pl.when(qi == 0)
    def _():
        lane = lax.broadcasted_iota(jnp.int32, (skv, LANES), 1)
        ak = augk_ref[0]
        zero = jnp.zeros_like(ak)
        for g in range(pairs):
            kp = k_ref[0, :, g * LANES:(g + 1) * LANES].astype(_BF16)
            for j in range(HEADS_PER_BLOCK):
                hh = g * HEADS_PER_BLOCK + j
                head = (step * pairs + g) * HEADS_PER_BLOCK + j
                kk_ref[hh, :, :LANES] = jnp.where(lane // HEAD_DIM == j, kp, zero)
                kk_ref[hh, :, LANES:] = jnp.where(lane % n_heads == head, ak, zero)
            vv_ref[g] = v_ref[0, :, g * LANES:(g + 1) * LANES].astype(_BF16)

    augq = augq_ref[0]
    q2 = [jnp.concatenate([q_ref[0, :, g * LANES:(g + 1) * LANES], augq], axis=1) for g in range(pairs)]
    q_start = past + qi * tq
    n_full = q_start // tk

    def lane_groups(width):
        return [(l0, min(LANES, width - l0)) for l0 in range(0, width, LANES)]

    band = DIAG_BAND if tq % DIAG_BAND == 0 else tq
    diag_start = pl.multiple_of(q_start, tq)
    bands = [(r0, r0 + band) for r0 in range(0, tq, band)]

    def scores(c, k_start, row0, rows, width, diag):
        for g, j in heads:
            ks = kk_ref[j, pl.ds(k_start, width), :]
            s = lax.dot_general(q2[g][row0:row0 + rows], ks, (((1,), (1,)), ((), ())),
                                preferred_element_type=_F32)
            if diag:
                tail = s[:, width - rows:]
                r = lax.broadcasted_iota(jnp.int32, tail.shape, 0)
                col = lax.broadcasted_iota(jnp.int32, tail.shape, 1)
                tail = jnp.where(col <= r, tail, NEG_INF)
                s = tail if width == rows else jnp.concatenate([s[:, :width - rows], tail], axis=1)
            s_ref[j, c, row0:row0 + rows, :width] = s
            for l0, gw in lane_groups(width):
                m_ref[j, row0:row0 + rows, :gw] = jnp.maximum(m_ref[j, row0:row0 + rows, :gw], s[:, l0:l0 + gw])

    m_ref[...] = jnp.full(m_ref.shape, NEG_INF, _F32)

    def pass1(c, carry):
        scores(c, pl.multiple_of(c * tk, tk), 0, tq, tk, False)
        return carry
    lax.fori_loop(0, n_full, pass1, 0)
    for row0, width in bands:
        scores(n_full, diag_start, row0, band, width, True)

    for _, j in heads:
        row_max = jnp.max(m_ref[j], axis=-1, keepdims=True)
        m_ref[j] = jnp.broadcast_to(row_max, (tq, LANES))
    l_ref[...] = jnp.zeros(l_ref.shape, _F32)
    acc_ref[...] = jnp.zeros(acc_ref.shape, _F32)

    def weights(c, k_start, row0, rows, width):
        for g, j in heads:
            pieces = []
            for l0, gw in lane_groups(width):
                p = jnp.exp2(s_ref[j, c, row0:row0 + rows, l0:l0 + gw] - m_ref[j, row0:row0 + rows, :gw])
                l_ref[j, row0:row0 + rows, :gw] += p
                pieces.append(p.astype(_BF16))
            p_all = pieces[0] if len(pieces) == 1 else jnp.concatenate(pieces, axis=1)
            acc_ref[j, row0:row0 + rows, :] += jnp.dot(p_all, vv_ref[g, pl.ds(k_start, width), :],
                                                       preferred_element_type=_F32)

    def pass2(c, carry):
        weights(c, pl.multiple_of(c * tk, tk), 0, tq, tk)
        return carry
    lax.fori_loop(0, n_full, pass2, 0)
    for row0, width in bands:
        weights(n_full, diag_start, row0, band, width)

    out_lane = lax.broadcasted_iota(jnp.int32, (tq, LANES), 1)
    for g in range(pairs):
        out = jnp.zeros((tq, LANES), _F32)
        for j in range(HEADS_PER_BLOCK):
            hh = g * HEADS_PER_BLOCK + j
            o = acc_ref[hh] / jnp.sum(l_ref[hh], axis=-1, keepdims=True)
            out = jnp.where(out_lane // HEAD_DIM == j, o, out)
        cols = slice(g * LANES, (g + 1) * LANES)
        a_ref[0, :, cols] = (out * sz_ref[0, :, cols].astype(_F32)).astype(_BF16)


def _attention(q, augq, sz, k_all, v_all, augk, *, tq, tk, n_heads, pairs):
    b, sq, a = q.shape
    skv = k_all.shape[1]
    past = skv - sq
    n_q = sq // tq
    width = pairs * LANES
    assert sq % tq == 0 and past % tk == 0 and (tq % tk == 0 or n_q == 1) and a == n_heads * HEAD_DIM
    assert a % width == 0
    n_chunks = (past + (n_q - 1) * tq) // tk + 1
    n_local = pairs * HEADS_PER_BLOCK
    return pl.pallas_call(
        functools.partial(_attention_kernel, past=past, tk=tk, n_heads=n_heads, pairs=pairs),
        grid=(b, a // width, n_q),
        in_specs=[
            pl.BlockSpec((1, tq, width), lambda i, p, t: (i, t, p)),
            pl.BlockSpec((1, tq, LANES), lambda i, p, t: (i, t, 0)),
            pl.BlockSpec((1, tq, width), lambda i, p, t: (i, t, p)),
            pl.BlockSpec((1, skv, width), lambda i, p, t: (i, 0, p)),
            pl.BlockSpec((1, skv, width), lambda i, p, t: (i, 0, p)),
            pl.BlockSpec((1, skv, LANES), lambda i, p, t: (i, 0, 0)),
        ],
        out_specs=pl.BlockSpec((1, tq, width), lambda i, p, t: (i, t, p)),
        out_shape=jax.ShapeDtypeStruct((b, sq, a), _BF16),
        scratch_shapes=[
            pltpu.VMEM((n_local, skv, 2 * LANES), _BF16),
            pltpu.VMEM((pairs, skv, LANES), _BF16),
            pltpu.VMEM((n_local, n_chunks, tq, max(tk, tq)), _F32),
            pltpu.VMEM((n_local, tq, LANES), _F32),
            pltpu.VMEM((n_local, tq, LANES), _F32),
            pltpu.VMEM((n_local, tq, LANES), _F32),
        ],
        compiler_params=_params("arbitrary", "arbitrary", "arbitrary"),
        name="attention",
    )(q, augq, sz, k_all, v_all, augk)


def _out_kernel(a_ref, x_ref, mod_ref, w_ref, g_ref, y_ref):
    o = jnp.dot(a_ref[0], w_ref[...], preferred_element_type=_F32)
    x = x_ref[0] + mod_ref[0, 2:3, :] * o
    ms = jnp.mean(x * x, axis=-1, keepdims=True)
    y_ref[0] = x * lax.rsqrt(ms + RMS_EPS) * g_ref[...]


def _out_layer(a, x, mod, w_out, final_g, *, tm):
    b, s, d = x.shape
    tok = lambda width: pl.BlockSpec((1, tm, width), lambda i, t: (i, t, 0))
    return pl.pallas_call(
        _out_kernel,
        grid=(b, s // tm),
        in_specs=[
            tok(a.shape[2]),
            tok(d),
            pl.BlockSpec((1, 3, d), lambda i, t: (i, 0, 0)),
            _const_spec(w_out.shape),
            _const_spec((1, d)),
        ],
        out_specs=tok(d),
        out_shape=jax.ShapeDtypeStruct((b, s, d), _F32),
        compiler_params=_params("arbitrary", "arbitrary"),
        name="out_layer",
    )(a, x, mod, w_out, final_g)


def _stream(x, mod0, mod1, hist, past_k, past_v, past_logf, w, *, tm, tq, tk):
    b, s, d = x.shape
    n_heads = w["n_heads"]
    x1, new_hist = _conv_layer(x, mod0, w["g0"], hist, w["conv_w_in"], w["conv_k"], w["conv_w_out"], tm=tm)
    q, k, v, sz, logf, logfx = _attn_proj(x1, mod1, w["g1"], w["attn_w_qkvz"], w["attn_wf_x"], w["attn_bf_x"],
                                          tm=tm, n_heads=n_heads)
    if past_k is None:
        k_all, v_all, logfx_all = k, v, logfx
    else:
        k_all = jnp.concatenate([past_k, k], axis=1)
        v_all = jnp.concatenate([past_v, v], axis=1)
        logfx_all = jnp.concatenate([jnp.tile(past_logf, (1, 1, LANES // n_heads)), logfx], axis=1)
    augq, augk = _forget_scan(logfx_all, sq=s, n_heads=n_heads)
    a = _attention(q, augq, sz, k_all, v_all, augk, tq=tq, tk=tk, n_heads=n_heads, pairs=ATTN_PAIRS_PER_STEP)
    y = _out_layer(a, x1, mod1, w["attn_w_out"], w["final_g"], tm=tm)
    return y, new_hist, k, v, logf


def kernel(x_prompt, x_sample, c_prompt, c_sample, state_conv, cache_k, cache_v, cache_logf, norm_g, ada_w, ada_b,
           conv_w_in, conv_k, conv_w_out, attn_w_in, attn_b_f, attn_w_out, final_g):
    bp, sp, d = x_prompt.shape
    bs, ss, _ = x_sample.shape
    n_heads = attn_b_f.shape[1]
    a = n_heads * HEAD_DIM
    past = cache_k.shape[2]
    assert norm_g.shape[0] == 2 and state_conv.shape[0] == 1 and cache_k.shape[0] == 1

    mod = _modulation(jnp.concatenate([c_prompt, c_sample], axis=0), ada_w, ada_b)
    mod = mod.reshape(2, bp + bs, 3, d)

    w_attn = attn_w_in[0]
    q_scale = jnp.concatenate([jnp.full((a,), LOG2_E / math.sqrt(HEAD_DIM), _F32), jnp.ones((3 * a,), _F32)])
    w = {
        "n_heads": n_heads,
        "g0": norm_g[0:1],
        "g1": norm_g[1:2],
        "conv_w_in": conv_w_in[0].astype(_BF16),
        "conv_k": conv_k[0],
        "conv_w_out": conv_w_out[0].astype(_BF16),
        "attn_w_qkvz": (w_attn[:, :4 * a] * q_scale).astype(_BF16),
        "attn_wf_x": jnp.tile(w_attn[:, 4 * a:], (1, LANES // n_heads)).astype(_BF16),
        "attn_bf_x": jnp.tile(attn_b_f, (1, LANES // n_heads)),
        "attn_w_out": attn_w_out[0].astype(_BF16),
        "final_g": final_g.reshape(1, d),
    }

    zero_hist = jnp.zeros((bp, CONV_TAPS - 1, conv_k.shape[2]), _F32)
    yp, hist_p, kp, vp, lfp = _stream(x_prompt, mod[0, :bp], mod[1, :bp], zero_hist, None, None, None, w,
                                      tm=512, tq=512, tk=512)
    ys, hist_s, ks, vs, lfs = _stream(x_sample, mod[0, bp:], mod[1, bp:], state_conv[0],
                                      cache_k[0].reshape(bs, past, a), cache_v[0].reshape(bs, past, a),
                                      cache_logf[0], w, tm=ss, tq=ss, tk=past)

    heads = lambda t: t.reshape(1, t.shape[0], t.shape[1], n_heads, HEAD_DIM)
    return (yp, ys, hist_p[None], heads(kp), heads(vp), lfp[None],
            hist_s[None], heads(ks), heads(vs), lfs[None])
```

```python
import functools
import math

import jax
import jax.numpy as jnp
from jax import lax
from jax.experimental import pallas as pl
from jax.experimental.pallas import tpu as pltpu

RMS_EPS = 1e-6
NEG_INF = -1e30
CONV_TAPS = 3
HEAD_DIM = 64
LANES = 128
HEADS_PER_BLOCK = LANES // HEAD_DIM
AUG_PARTS = 3
DIAG_BAND = 256
LOG2_E = math.log2(math.e)
ATTN_PAIRS_PER_STEP = 2
VMEM_LIMIT_BYTES = 56 * 1024 * 1024

_F32 = jnp.float32
_BF16 = jnp.bfloat16


def _const_spec(shape):
    return pl.BlockSpec(shape, lambda *_: (0,) * len(shape), pipeline_mode=pl.Buffered(1))


def _params(*semantics):
    return pltpu.CompilerParams(dimension_semantics=semantics, vmem_limit_bytes=VMEM_LIMIT_BYTES)


def _silu(x):
    return x / (1.0 + jnp.exp(-x))


def _ada_norm(x, mod_ref, g_ref):
    ms = jnp.mean(x * x, axis=-1, keepdims=True)
    y = x * lax.rsqrt(ms + RMS_EPS) * g_ref[...]
    return y * (1.0 + mod_ref[0, 1:2, :]) + mod_ref[0, 0:1, :]


def _mod_kernel(c_ref, w_ref, b_ref, o_ref):
    c = c_ref[...]
    o_ref[0] = jnp.dot(_silu(c), w_ref[0], preferred_element_type=_F32,
                       precision=lax.Precision.HIGHEST) + b_ref[0]


def _modulation(c_all, ada_w, ada_b):
    depth, d, d3 = ada_w.shape
    n = c_all.shape[0]
    col = d
    return pl.pallas_call(
        _mod_kernel,
        grid=(depth, d3 // col),
        in_specs=[
            pl.BlockSpec((n, d), lambda i, j: (0, 0)),
            pl.BlockSpec((1, d, col), lambda i, j: (i, 0, j)),
            pl.BlockSpec((1, 1, col), lambda i, j: (i, 0, j)),
        ],
        out_specs=pl.BlockSpec((1, n, col), lambda i, j: (i, 0, j)),
        out_shape=jax.ShapeDtypeStruct((depth, n, d3), _F32),
        compiler_params=_params("arbitrary", "arbitrary"),
        name="modulation",
    )(c_all, ada_w, ada_b.reshape(depth, 1, d3))


def _conv_kernel(x_ref, mod_ref, g_ref, hist_ref, win_ref, ck_ref, wout_ref,
                 xo_ref, hist_o_ref, carry_ref, *, e_chunk):
    t = pl.program_id(1)
    tm = x_ref.shape[1]
    e_width = ck_ref.shape[1]

    @pl.when(t == 0)
    def _():
        carry_ref[...] = hist_ref[0]

    x = x_ref[0]
    h = _ada_norm(x, mod_ref, g_ref).astype(_BF16)
    row = lax.broadcasted_iota(jnp.int32, (tm, e_chunk), 0)
    acc = jnp.zeros(x.shape, _F32)
    for e0 in range(0, e_width, e_chunk):
        def proj(branch):
            lo = branch * e_width + e0
            return jnp.dot(h, win_ref[:, lo:lo + e_chunk], preferred_element_type=_F32)
        bg, cg, xv, z = proj(0), proj(1), proj(2), proj(3)
        u = cg * xv
        prev2 = carry_ref[0:1, e0:e0 + e_chunk]
        prev1 = carry_ref[1:2, e0:e0 + e_chunk]
        u1 = jnp.where(row == 0, prev1, pltpu.roll(u, 1, 0))
        u2 = jnp.where(row == 0, prev2, jnp.where(row == 1, prev1, pltpu.roll(u, 2, 0)))
        conv = (ck_ref[0:1, e0:e0 + e_chunk] * u2 + ck_ref[1:2, e0:e0 + e_chunk] * u1
                + ck_ref[2:3, e0:e0 + e_chunk] * u)
        y = bg * conv * _silu(z)
        acc = acc + jnp.dot(y.astype(_BF16), wout_ref[e0:e0 + e_chunk, :], preferred_element_type=_F32)
        carry_ref[:, e0:e0 + e_chunk] = u[tm - (CONV_TAPS - 1):, :]
    xo_ref[0] = x + mod_ref[0, 2:3, :] * acc
    hist_o_ref[0] = carry_ref[...]


def _conv_layer(x, mod, g, hist, w_in, conv_k, w_out, *, tm):
    b, s, d = x.shape
    e_width = conv_k.shape[1]
    e_chunk = min(512, e_width)
    assert s % tm == 0 and tm >= CONV_TAPS - 1 and e_width % e_chunk == 0
    return pl.pallas_call(
        functools.partial(_conv_kernel, e_chunk=e_chunk),
        grid=(b, s // tm),
        in_specs=[
            pl.BlockSpec((1, tm, d), lambda i, t: (i, t, 0)),
            pl.BlockSpec((1, 3, d), lambda i, t: (i, 0, 0)),
            _const_spec((1, d)),
            pl.BlockSpec((1, CONV_TAPS - 1, e_width), lambda i, t: (i, 0, 0)),
            _const_spec(w_in.shape),
            _const_spec(conv_k.shape),
            _const_spec(w_out.shape),
        ],
        out_specs=[
            pl.BlockSpec((1, tm, d), lambda i, t: (i, t, 0)),
            pl.BlockSpec((1, CONV_TAPS - 1, e_width), lambda i, t: (i, 0, 0)),
        ],
        out_shape=[
            jax.ShapeDtypeStruct((b, s, d), _F32),
            jax.ShapeDtypeStruct((b, CONV_TAPS - 1, e_width), _F32),
        ],
        scratch_shapes=[pltpu.VMEM((CONV_TAPS - 1, e_width), _F32)],
        compiler_params=_params("arbitrary", "arbitrary"),
        name="conv_layer",
    )(x, mod, g, hist, w_in, conv_k, w_out)


def _attn_proj_kernel(x_ref, mod_ref, g_ref, w_ref, wf_ref, bf_ref,
                      q_ref, k_ref, v_ref, sz_ref, logf_ref, logfx_ref):
    a = q_ref.shape[2]
    n_heads = logf_ref.shape[2]
    h = _ada_norm(x_ref[0], mod_ref, g_ref).astype(_BF16)

    def proj(i):
        return jnp.dot(h, w_ref[:, i * a:(i + 1) * a], preferred_element_type=_F32)

    q_ref[0] = proj(0).astype(_BF16)
    k_ref[0] = proj(1)
    v_ref[0] = proj(2)
    sz_ref[0] = _silu(proj(3)).astype(_BF16)
    f = jnp.dot(h, wf_ref[...], preferred_element_type=_F32) + bf_ref[...]
    logf = jnp.minimum(f, 0.0) - jnp.log1p(jnp.exp(-jnp.abs(f)))
    logfx_ref[0] = logf
    logf_ref[0] = logf[:, :n_heads]


def _attn_proj(x, mod, g, w_qkvz, wf_x, bf_x, *, tm, n_heads):
    b, s, d = x.shape
    a = w_qkvz.shape[1] // 4
    assert s % tm == 0
    tok = lambda width: pl.BlockSpec((1, tm, width), lambda i, t: (i, t, 0))
    return pl.pallas_call(
        _attn_proj_kernel,
        grid=(b, s // tm),
        in_specs=[
            tok(d),
            pl.BlockSpec((1, 3, d), lambda i, t: (i, 0, 0)),
            _const_spec((1, d)),
            _const_spec(w_qkvz.shape),
            _const_spec(wf_x.shape),
            _const_spec(bf_x.shape),
        ],
        out_specs=[tok(a), tok(a), tok(a), tok(a), tok(n_heads), tok(LANES)],
        out_shape=[
            jax.ShapeDtypeStruct((b, s, a), _BF16),
            jax.ShapeDtypeStruct((b, s, a), _F32),
            jax.ShapeDtypeStruct((b, s, a), _F32),
            jax.ShapeDtypeStruct((b, s, a), _BF16),
            jax.ShapeDtypeStruct((b, s, n_heads), _F32),
            jax.ShapeDtypeStruct((b, s, LANES), _F32),
        ],
        compiler_params=_params("arbitrary", "arbitrary"),
        name="attn_proj",
    )(x, mod, g, w_qkvz, wf_x, bf_x)


def _split_bf16(c):
    parts = []
    rest = c
    for _ in range(AUG_PARTS):
        piece = rest.astype(_BF16).astype(_F32)
        parts.append(piece)
        rest = rest - piece
    return parts


def _scan_kernel(logfx_ref, augq_ref, augk_ref, *, n_heads):
    skv = logfx_ref.shape[1]
    sq = augq_ref.shape[1]
    c = logfx_ref[0]
    row = lax.broadcasted_iota(jnp.int32, c.shape, 0)
    shift = 1
    while shift < skv:
        c = c + jnp.where(row >= shift, pltpu.roll(c, shift, 0), 0.0)
        shift *= 2
    parts = _split_bf16(c * LOG2_E)
    col = lax.broadcasted_iota(jnp.int32, c.shape, 1) // n_heads
    aq = jnp.zeros_like(c)
    ak = jnp.zeros_like(c)
    for i, piece in enumerate(parts):
        aq = jnp.where(col == i, piece, aq)
        ak = jnp.where(col == AUG_PARTS + i, -piece, ak)
    aq = jnp.where((col >= AUG_PARTS) & (col < 2 * AUG_PARTS), 1.0, aq)
    ak = jnp.where(col < AUG_PARTS, 1.0, ak)
    augk_ref[0] = ak.astype(_BF16)
    augq_ref[0] = aq[skv - sq:, :].astype(_BF16)


def _forget_scan(logfx_all, *, sq, n_heads):
    b, skv, lanes = logfx_all.shape
    assert lanes == LANES and 2 * AUG_PARTS * n_heads <= LANES
    return pl.pallas_call(
        functools.partial(_scan_kernel, n_heads=n_heads),
        grid=(b,),
        in_specs=[pl.BlockSpec((1, skv, LANES), lambda i: (i, 0, 0))],
        out_specs=[
            pl.BlockSpec((1, sq, LANES), lambda i: (i, 0, 0)),
            pl.BlockSpec((1, skv, LANES), lambda i: (i, 0, 0)),
        ],
        out_shape=[
            jax.ShapeDtypeStruct((b, sq, LANES), _BF16),
            jax.ShapeDtypeStruct((b, skv, LANES), _BF16),
        ],
        compiler_params=_params("arbitrary"),
        name="forget_scan",
    )(logfx_all)


def _attention_kernel(q_ref, augq_ref, sz_ref, k_ref, v_ref, augk_ref, a_ref,
                      kk_ref, vv_ref, s_ref, m_ref, l_ref, acc_ref, *, past, tk, n_heads, pairs):
    step = pl.program_id(1)
    qi = pl.program_id(2)
    tq = q_ref.shape[1]
    skv = k_ref.shape[1]
    heads = [(g, g * HEADS_PER_BLOCK + j) for g in range(pairs) for j in range(HEADS_PER_BLOCK)]

    @pl.when(qi == 0)
    def _():
        lane = lax.broadcasted_iota(jnp.int32, (skv, LANES), 1)
        ak = augk_ref[0]
        zero = jnp.zeros_like(ak)
        for g in range(pairs):
            kp = k_ref[0, :, g * LANES:(g + 1) * LANES].astype(_BF16)
            for j in range(HEADS_PER_BLOCK):
                hh = g * HEADS_PER_BLOCK + j
                head = (step * pairs + g) * HEADS_PER_BLOCK + j
                kk_ref[hh, :, :LANES] = jnp.where(lane // HEAD_DIM == j, kp, zero)
                kk_ref[hh, :, LANES:] = jnp.where(lane % n_heads == head, ak, zero)
            vv_ref[g] = v_ref[0, :, g * LANES:(g + 1) * LANES].astype(_BF16)

    augq = augq_ref[0]
    q2 = [jnp.concatenate([q_ref[0, :, g * LANES:(g + 1) * LANES], augq], axis=1) for g in range(pairs)]
    q_start = past + qi * tq
    n_full = q_start // tk

    def lane_groups(width):
        return [(l0, min(LANES, width - l0)) for l0 in range(0, width, LANES)]

    band = DIAG_BAND if tq % DIAG_BAND == 0 else tq
    diag_start = pl.multiple_of(q_start, tq)
    bands = [(r0, r0 + band) for r0 in range(0, tq, band)]

    def scores(c, k_start, row0, rows, width, diag):
        for g, j in heads:
            ks = kk_ref[j, pl.ds(k_start, width), :]
            s = lax.dot_general(q2[g][row0:row0 + rows], ks, (((1,), (1,)), ((), ())),
                                preferred_element_type=_F32)
            if diag:
                tail = s[:, width - rows:]
                r = lax.broadcasted_iota(jnp.int32, tail.shape, 0)
                col = lax.broadcasted_iota(jnp.int32, tail.shape, 1)
                tail = jnp.where(col <= r, tail, NEG_INF)
                s = tail if width == rows else jnp.concatenate([s[:, :width - rows], tail], axis=1)
            s_ref[j, c, row0:row0 + rows, :width] = s
            fold(m_ref, j, row0, rows, [(gw, s[:, l0:l0 + gw]) for l0, gw in lane_groups(width)], jnp.maximum, diag)

    def fold(ref, j, row0, rows, groups, op, assign):
        if lane_aligned:
            val = functools.reduce(op, [val for _, val in groups])
            ref[j, row0:row0 + rows, :] = val if assign else op(ref[j, row0:row0 + rows, :], val)
        else:
            for gw, val in groups:
                ref[j, row0:row0 + rows, :gw] = op(ref[j, row0:row0 + rows, :gw], val)

    lane_aligned = band % LANES == 0
    if not lane_aligned:
        m_ref[...] = jnp.full(m_ref.shape, NEG_INF, _F32)
        l_ref[...] = jnp.zeros(l_ref.shape, _F32)

    for row0, width in bands:
        scores(n_full, diag_start, row0, band, width, True)

    def pass1(c, carry):
        scores(c, pl.multiple_of(c * tk, tk), 0, tq, tk, False)
        return carry
    lax.fori_loop(0, n_full, pass1, 0)

    for _, j in heads:
        row_max = jnp.max(m_ref[j], axis=-1, keepdims=True)
        m_ref[j] = jnp.broadcast_to(row_max, (tq, LANES))

    def weights(c, k_start, row0, rows, width, diag):
        for g, j in heads:
            groups = []
            for l0, gw in lane_groups(width):
                p = jnp.exp2(s_ref[j, c, row0:row0 + rows, l0:l0 + gw] - m_ref[j, row0:row0 + rows, :gw])
                groups.append((gw, p))
            fold(l_ref, j, row0, rows, groups, jnp.add, diag)
            pieces = [p.astype(_BF16) for _, p in groups]
            p_all = pieces[0] if len(pieces) == 1 else jnp.concatenate(pieces, axis=1)
            pv = jnp.dot(p_all, vv_ref[g, pl.ds(k_start, width), :], preferred_element_type=_F32)
            if diag:
                acc_ref[j, row0:row0 + rows, :] = pv
            else:
                acc_ref[j, row0:row0 + rows, :] += pv

    for row0, width in bands:
        weights(n_full, diag_start, row0, band, width, True)

    def pass2(c, carry):
        weights(c, pl.multiple_of(c * tk, tk), 0, tq, tk, False)
        return carry
    lax.fori_loop(0, n_full, pass2, 0)

    out_lane = lax.broadcasted_iota(jnp.int32, (tq, LANES), 1)
    for g in range(pairs):
        out = jnp.zeros((tq, LANES), _F32)
        for j in range(HEADS_PER_BLOCK):
            hh = g * HEADS_PER_BLOCK + j
            o = acc_ref[hh] / jnp.sum(l_ref[hh], axis=-1, keepdims=True)
            out = jnp.where(out_lane // HEAD_DIM == j, o, out)
        cols = slice(g * LANES, (g + 1) * LANES)
        a_ref[0, :, cols] = (out * sz_ref[0, :, cols].astype(_F32)).astype(_BF16)


def _attention(q, augq, sz, k_all, v_all, augk, *, tq, tk, n_heads, pairs):
    b, sq, a = q.shape
    skv = k_all.shape[1]
    past = skv - sq
    n_q = sq // tq
    width = pairs * LANES
    assert sq % tq == 0 and past % tk == 0 and (tq % tk == 0 or n_q == 1) and a == n_heads * HEAD_DIM
    assert a % width == 0
    n_chunks = (past + (n_q - 1) * tq) // tk + 1
    n_local = pairs * HEADS_PER_BLOCK
    return pl.pallas_call(
        functools.partial(_attention_kernel, past=past, tk=tk, n_heads=n_heads, pairs=pairs),
        grid=(b, a // width, n_q),
        in_specs=[
            pl.BlockSpec((1, tq, width), lambda i, p, t: (i, t, p)),
            pl.BlockSpec((1, tq, LANES), lambda i, p, t: (i, t, 0)),
            pl.BlockSpec((1, tq, width), lambda i, p, t: (i, t, p)),
            pl.BlockSpec((1, skv, width), lambda i, p, t: (i, 0, p)),
            pl.BlockSpec((1, skv, width), lambda i, p, t: (i, 0, p)),
            pl.BlockSpec((1, skv, LANES), lambda i, p, t: (i, 0, 0)),
        ],
        out_specs=pl.BlockSpec((1, tq, width), lambda i, p, t: (i, t, p)),
        out_shape=jax.ShapeDtypeStruct((b, sq, a), _BF16),
        scratch_shapes=[
            pltpu.VMEM((n_local, skv, 2 * LANES), _BF16),
            pltpu.VMEM((pairs, skv, LANES), _BF16),
            pltpu.VMEM((n_local, n_chunks, tq, max(tk, tq)), _F32),
            pltpu.VMEM((n_local, tq, LANES), _F32),
            pltpu.VMEM((n_local, tq, LANES), _F32),
            pltpu.VMEM((n_local, tq, LANES), _F32),
        ],
        compiler_params=_params("arbitrary", "arbitrary", "arbitrary"),
        name="attention",
    )(q, augq, sz, k_all, v_all, augk)


def _out_kernel(a_ref, x_ref, mod_ref, w_ref, g_ref, y_ref):
    o = jnp.dot(a_ref[0], w_ref[...], preferred_element_type=_F32)
    x = x_ref[0] + mod_ref[0, 2:3, :] * o
    ms = jnp.mean(x * x, axis=-1, keepdims=True)
    y_ref[0] = x * lax.rsqrt(ms + RMS_EPS) * g_ref[...]


def _out_layer(a, x, mod, w_out, final_g, *, tm):
    b, s, d = x.shape
    tok = lambda width: pl.BlockSpec((1, tm, width), lambda i, t: (i, t, 0))
    return pl.pallas_call(
        _out_kernel,
        grid=(b, s // tm),
        in_specs=[
            tok(a.shape[2]),
            tok(d),
            pl.BlockSpec((1, 3, d), lambda i, t: (i, 0, 0)),
            _const_spec(w_out.shape),
            _const_spec((1, d)),
        ],
        out_specs=tok(d),
        out_shape=jax.ShapeDtypeStruct((b, s, d), _F32),
        compiler_params=_params("arbitrary", "arbitrary"),
        name="out_layer",
    )(a, x, mod, w_out, final_g)


def _stream(x, mod0, mod1, hist, past_k, past_v, past_logf, w, *, tm, tq, tk):
    b, s, d = x.shape
    n_heads = w["n_heads"]
    x1, new_hist = _conv_layer(x, mod0, w["g0"], hist, w["conv_w_in"], w["conv_k"], w["conv_w_out"], tm=tm)
    q, k, v, sz, logf, logfx = _attn_proj(x1, mod1, w["g1"], w["attn_w_qkvz"], w["attn_wf_x"], w["attn_bf_x"],
                                          tm=tm, n_heads=n_heads)
    if past_k is None:
        k_all, v_all, logfx_all = k, v, logfx
    else:
        k_all = jnp.concatenate([past_k, k], axis=1)
        v_all = jnp.concatenate([past_v, v], axis=1)
        logfx_all = jnp.concatenate([jnp.tile(past_logf, (1, 1, LANES // n_heads)), logfx], axis=1)
    augq, augk = _forget_scan(logfx_all, sq=s, n_heads=n_heads)
    a = _attention(q, augq, sz, k_all, v_all, augk, tq=tq, tk=tk, n_heads=n_heads, pairs=ATTN_PAIRS_PER_STEP)
    y = _out_layer(a, x1, mod1, w["attn_w_out"], w["final_g"], tm=tm)
    return y, new_hist, k, v, logf


def kernel(x_prompt, x_sample, c_prompt, c_sample, state_conv, cache_k, cache_v, cache_logf, norm_g, ada_w, ada_b,
           conv_w_in, conv_k, conv_w_out, attn_w_in, attn_b_f, attn_w_out, final_g):
    bp, sp, d = x_prompt.shape
    bs, ss, _ = x_sample.shape
    n_heads = attn_b_f.shape[1]
    a = n_heads * HEAD_DIM
    past = cache_k.shape[2]
    assert norm_g.shape[0] == 2 and state_conv.shape[0] == 1 and cache_k.shape[0] == 1

    mod = _modulation(jnp.concatenate([c_prompt, c_sample], axis=0), ada_w, ada_b)
    mod = mod.reshape(2, bp + bs, 3, d)

    w_attn = attn_w_in[0]
    q_scale = jnp.concatenate([jnp.full((a,), LOG2_E / math.sqrt(HEAD_DIM), _F32), jnp.ones((3 * a,), _F32)])
    w = {
        "n_heads": n_heads,
        "g0": norm_g[0:1],
        "g1": norm_g[1:2],
        "conv_w_in": conv_w_in[0].astype(_BF16),
        "conv_k": conv_k[0],
        "conv_w_out": conv_w_out[0].astype(_BF16),
        "attn_w_qkvz": (w_attn[:, :4 * a] * q_scale).astype(_BF16),
        "attn_wf_x": jnp.tile(w_attn[:, 4 * a:], (1, LANES // n_heads)).astype(_BF16),
        "attn_bf_x": jnp.tile(attn_b_f, (1, LANES // n_heads)),
        "attn_w_out": attn_w_out[0].astype(_BF16),
        "final_g": final_g.reshape(1, d),
    }

    zero_hist = jnp.zeros((bp, CONV_TAPS - 1, conv_k.shape[2]), _F32)
    yp, hist_p, kp, vp, lfp = _stream(x_prompt, mod[0, :bp], mod[1, :bp], zero_hist, None, None, None, w,
                                      tm=512, tq=512, tk=512)
    ys, hist_s, ks, vs, lfs = _stream(x_sample, mod[0, bp:], mod[1, bp:], state_conv[0],
                                      cache_k[0].reshape(bs, past, a), cache_v[0].reshape(bs, past, a),
                                      cache_logf[0], w, tm=ss, tq=ss, tk=past)

    heads = lambda t: t.reshape(1, t.shape[0], t.shape[1], n_heads, HEAD_DIM)
    return (yp, ys, hist_p[None], heads(kp), heads(vp), lfp[None],
            hist_s[None], heads(ks), heads(vs), lfs[None])
```

```python
import functools
import math

import jax
import jax.numpy as jnp
from jax import lax
from jax.experimental import pallas as pl
from jax.experimental.pallas import tpu as pltpu

RMS_EPS = 1e-6
NEG_INF = -1e30
CONV_TAPS = 3
HEAD_DIM = 64
LANES = 128
HEADS_PER_BLOCK = LANES // HEAD_DIM
AUG_PARTS = 3
DIAG_BAND = 256
LOG2_E = math.log2(math.e)
ATTN_PAIRS_PER_STEP = 2
VMEM_LIMIT_BYTES = 56 * 1024 * 1024

_F32 = jnp.float32
_BF16 = jnp.bfloat16


def _const_spec(shape):
    return pl.BlockSpec(shape, lambda *_: (0,) * len(shape), pipeline_mode=pl.Buffered(1))


def _params(*semantics):
    return pltpu.CompilerParams(dimension_semantics=semantics, vmem_limit_bytes=VMEM_LIMIT_BYTES)


def _silu(x):
    return x / (1.0 + jnp.exp(-x))


def _ada_norm(x, mod_ref, g_ref):
    ms = jnp.mean(x * x, axis=-1, keepdims=True)
    y = x * lax.rsqrt(ms + RMS_EPS) * g_ref[...]
    return y * (1.0 + mod_ref[0, 1:2, :]) + mod_ref[0, 0:1, :]


def _mod_kernel(c_ref, w_ref, b_ref, o_ref):
    c = c_ref[...]
    o_ref[0] = jnp.dot(_silu(c), w_ref[0], preferred_element_type=_F32,
                       precision=lax.Precision.HIGHEST) + b_ref[0]


def _modulation(c_all, ada_w, ada_b):
    depth, d, d3 = ada_w.shape
    n = c_all.shape[0]
    col = d
    return pl.pallas_call(
        _mod_kernel,
        grid=(depth, d3 // col),
        in_specs=[
            pl.BlockSpec((n, d), lambda i, j: (0, 0)),
            pl.BlockSpec((1, d, col), lambda i, j: (i, 0, j)),
            pl.BlockSpec((1, 1, col), lambda i, j: (i, 0, j)),
        ],
        out_specs=pl.BlockSpec((1, n, col), lambda i, j: (i, 0, j)),
        out_shape=jax.ShapeDtypeStruct((depth, n, d3), _F32),
        compiler_params=_params("arbitrary", "arbitrary"),
        name="modulation",
    )(c_all, ada_w, ada_b.reshape(depth, 1, d3))


def _conv_kernel(x_ref, mod_ref, g_ref, hist_ref, win_ref, ck_ref, wout_ref,
                 xo_ref, hist_o_ref, carry_ref, *, e_chunk):
    t = pl.program_id(1)
    tm = x_ref.shape[1]
    e_width = ck_ref.shape[1]

    @pl.when(t == 0)
    def _():
        carry_ref[...] = hist_ref[0]

    x = x_ref[0]
    h = _ada_norm(x, mod_ref, g_ref).astype(_BF16)
    row = lax.broadcasted_iota(jnp.int32, (tm, e_chunk), 0)
    acc = jnp.zeros(x.shape, _F32)
    for e0 in range(0, e_width, e_chunk):
        def proj(branch):
            lo = branch * e_width + e0
            return jnp.dot(h, win_ref[:, lo:lo + e_chunk], preferred_element_type=_F32)
        bg, cg, xv, z = proj(0), proj(1), proj(2), proj(3)
        u = cg * xv
        prev2 = carry_ref[0:1, e0:e0 + e_chunk]
        prev1 = carry_ref[1:2, e0:e0 + e_chunk]
        u1 = jnp.where(row == 0, prev1, pltpu.roll(u, 1, 0))
        u2 = jnp.where(row == 0, prev2, jnp.where(row == 1, prev1, pltpu.roll(u, 2, 0)))
        conv = (ck_ref[0:1, e0:e0 + e_chunk] * u2 + ck_ref[1:2, e0:e0 + e_chunk] * u1
                + ck_ref[2:3, e0:e0 + e_chunk] * u)
        y = bg * conv * _silu(z)
        acc = acc + jnp.dot(y.astype(_BF16), wout_ref[e0:e0 + e_chunk, :], preferred_element_type=_F32)
        carry_ref[:, e0:e0 + e_chunk] = u[tm - (CONV_TAPS - 1):, :]
    xo_ref[0] = x + mod_ref[0, 2:3, :] * acc
    hist_o_ref[0] = carry_ref[...]


def _conv_layer(x, mod, g, hist, w_in, conv_k, w_out, *, tm):
    b, s, d = x.shape
    e_width = conv_k.shape[1]
    e_chunk = min(1024, e_width)
    assert s % tm == 0 and tm >= CONV_TAPS - 1 and e_width % e_chunk == 0
    return pl.pallas_call(
        functools.partial(_conv_kernel, e_chunk=e_chunk),
        grid=(b, s // tm),
        in_specs=[
            pl.BlockSpec((1, tm, d), lambda i, t: (i, t, 0)),
            pl.BlockSpec((1, 3, d), lambda i, t: (i, 0, 0)),
            _const_spec((1, d)),
            pl.BlockSpec((1, CONV_TAPS - 1, e_width), lambda i, t: (i, 0, 0)),
            _const_spec(w_in.shape),
            _const_spec(conv_k.shape),
            _const_spec(w_out.shape),
        ],
        out_specs=[
            pl.BlockSpec((1, tm, d), lambda i, t: (i, t, 0)),
            pl.BlockSpec((1, CONV_TAPS - 1, e_width), lambda i, t: (i, 0, 0)),
        ],
        out_shape=[
            jax.ShapeDtypeStruct((b, s, d), _F32),
            jax.ShapeDtypeStruct((b, CONV_TAPS - 1, e_width), _F32),
        ],
        scratch_shapes=[pltpu.VMEM((CONV_TAPS - 1, e_width), _F32)],
        compiler_params=_params("arbitrary", "arbitrary"),
        name="conv_layer",
    )(x, mod, g, hist, w_in, conv_k, w_out)


def _attn_proj_kernel(x_ref, mod_ref, g_ref, w_ref, wf_ref, bf_ref,
                      q_ref, k_ref, v_ref, sz_ref, logf_ref, logfx_ref):
    a = q_ref.shape[2]
    n_heads = logf_ref.shape[2]
    h = _ada_norm(x_ref[0], mod_ref, g_ref).astype(_BF16)

    def proj(i):
        return jnp.dot(h, w_ref[:, i * a:(i + 1) * a], preferred_element_type=_F32)

    q_ref[0] = proj(0).astype(_BF16)
    k_ref[0] = proj(1)
    v_ref[0] = proj(2)
    sz_ref[0] = _silu(proj(3)).astype(_BF16)
    f = jnp.dot(h, wf_ref[...], preferred_element_type=_F32) + bf_ref[...]
    logf = jnp.minimum(f, 0.0) - jnp.log1p(jnp.exp(-jnp.abs(f)))
    logfx_ref[0] = logf
    logf_ref[0] = logf[:, :n_heads]


def _attn_proj(x, mod, g, w_qkvz, wf_x, bf_x, *, tm, n_heads):
    b, s, d = x.shape
    a = w_qkvz.shape[1] // 4
    assert s % tm == 0
    tok = lambda width: pl.BlockSpec((1, tm, width), lambda i, t: (i, t, 0))
    return pl.pallas_call(
        _attn_proj_kernel,
        grid=(b, s // tm),
        in_specs=[
            tok(d),
            pl.BlockSpec((1, 3, d), lambda i, t: (i, 0, 0)),
            _const_spec((1, d)),
            _const_spec(w_qkvz.shape),
            _const_spec(wf_x.shape),
            _const_spec(bf_x.shape),
        ],
        out_specs=[tok(a), tok(a), tok(a), tok(a), tok(n_heads), tok(LANES)],
        out_shape=[
            jax.ShapeDtypeStruct((b, s, a), _BF16),
            jax.ShapeDtypeStruct((b, s, a), _F32),
            jax.ShapeDtypeStruct((b, s, a), _F32),
            jax.ShapeDtypeStruct((b, s, a), _BF16),
            jax.ShapeDtypeStruct((b, s, n_heads), _F32),
            jax.ShapeDtypeStruct((b, s, LANES), _F32),
        ],
        compiler_params=_params("arbitrary", "arbitrary"),
        name="attn_proj",
    )(x, mod, g, w_qkvz, wf_x, bf_x)


def _split_bf16(c):
    parts = []
    rest = c
    for _ in range(AUG_PARTS):
        piece = rest.astype(_BF16).astype(_F32)
        parts.append(piece)
        rest = rest - piece
    return parts


def _scan_kernel(logfx_ref, augq_ref, augk_ref, *, n_heads):
    skv = logfx_ref.shape[1]
    sq = augq_ref.shape[1]
    c = logfx_ref[0]
    row = lax.broadcasted_iota(jnp.int32, c.shape, 0)
    shift = 1
    while shift < skv:
        c = c + jnp.where(row >= shift, pltpu.roll(c, shift, 0), 0.0)
        shift *= 2
    parts = _split_bf16(c * LOG2_E)
    col = lax.broadcasted_iota(jnp.int32, c.shape, 1) // n_heads
    aq = jnp.zeros_like(c)
    ak = jnp.zeros_like(c)
    for i, piece in enumerate(parts):
        aq = jnp.where(col == i, piece, aq)
        ak = jnp.where(col == AUG_PARTS + i, -piece, ak)
    aq = jnp.where((col >= AUG_PARTS) & (col < 2 * AUG_PARTS), 1.0, aq)
    ak = jnp.where(col < AUG_PARTS, 1.0, ak)
    augk_ref[0] = ak.astype(_BF16)
    augq_ref[0] = aq[skv - sq:, :].astype(_BF16)


def _forget_scan(logfx_all, *, sq, n_heads):
    b, skv, lanes = logfx_all.shape
    assert lanes == LANES and 2 * AUG_PARTS * n_heads <= LANES
    return pl.pallas_call(
        functools.partial(_scan_kernel, n_heads=n_heads),
        grid=(b,),
        in_specs=[pl.BlockSpec((1, skv, LANES), lambda i: (i, 0, 0))],
        out_specs=[
            pl.BlockSpec((1, sq, LANES), lambda i: (i, 0, 0)),
            pl.BlockSpec((1, skv, LANES), lambda i: (i, 0, 0)),
        ],
        out_shape=[
            jax.ShapeDtypeStruct((b, sq, LANES), _BF16),
            jax.ShapeDtypeStruct((b, skv, LANES), _BF16),
        ],
        compiler_params=_params("arbitrary"),
        name="forget_scan",
    )(logfx_all)


def _attention_kernel(q_ref, augq_ref, sz_ref, k_ref, v_ref, augk_ref, a_ref,
                      kk_ref, vv_ref, s_ref, m_ref, l_ref, acc_ref, *, past, tk, n_heads, pairs):
    step = pl.program_id(1)
    qi = pl.program_id(2)
    tq = q_ref.shape[1]
    skv = k_ref.shape[1]
    heads = [(g, g * HEADS_PER_BLOCK + j) for g in range(pairs) for j in range(HEADS_PER_BLOCK)]

    @pl.when(qi == 0)
    def _():
        lane = lax.broadcasted_iota(jnp.int32, (skv, LANES), 1)
        ak = augk_ref[0]
        zero = jnp.zeros_like(ak)
        for g in range(pairs):
            kp = k_ref[0, :, g * LANES:(g + 1) * LANES].astype(_BF16)
            for j in range(HEADS_PER_BLOCK):
                hh = g * HEADS_PER_BLOCK + j
                head = (step * pairs + g) * HEADS_PER_BLOCK + j
                kk_ref[hh, :, :LANES] = jnp.where(lane // HEAD_DIM == j, kp, zero)
                kk_ref[hh, :, LANES:] = jnp.where(lane % n_heads == head, ak, zero)
            vv_ref[g] = v_ref[0, :, g * LANES:(g + 1) * LANES].astype(_BF16)

    augq = augq_ref[0]
    q2 = [jnp.concatenate([q_ref[0, :, g * LANES:(g + 1) * LANES], augq], axis=1) for g in range(pairs)]
    q_start = past + qi * tq
    n_full = q_start // tk

    def lane_groups(width):
        return [(l0, min(LANES, width - l0)) for l0 in range(0, width, LANES)]

    band = DIAG_BAND if tq % DIAG_BAND == 0 else tq
    diag_start = pl.multiple_of(q_start, tq)
    bands = [(r0, r0 + band) for r0 in range(0, tq, band)]

    def scores(c, k_start, row0, rows, width, diag):
        for g, j in heads:
            ks = kk_ref[j, pl.ds(k_start, width), :]
            s = lax.dot_general(q2[g][row0:row0 + rows], ks, (((1,), (1,)), ((), ())),
                                preferred_element_type=_F32)
            if diag:
                tail = s[:, width - rows:]
                r = lax.broadcasted_iota(jnp.int32, tail.shape, 0)
                col = lax.broadcasted_iota(jnp.int32, tail.shape, 1)
                tail = jnp.where(col <= r, tail, NEG_INF)
                s = tail if width == rows else jnp.concatenate([s[:, :width - rows], tail], axis=1)
            s_ref[j, c, row0:row0 + rows, :width] = s
            fold(m_ref, j, row0, rows, [(gw, s[:, l0:l0 + gw]) for l0, gw in lane_groups(width)], jnp.maximum, diag)

    def fold(ref, j, row0, rows, groups, op, assign):
        if lane_aligned:
            val = functools.reduce(op, [val for _, val in groups])
            ref[j, row0:row0 + rows, :] = val if assign else op(ref[j, row0:row0 + rows, :], val)
        else:
            for gw, val in groups:
                ref[j, row0:row0 + rows, :gw] = op(ref[j, row0:row0 + rows, :gw], val)

    lane_aligned = band % LANES == 0
    if not lane_aligned:
        m_ref[...] = jnp.full(m_ref.shape, NEG_INF, _F32)
        l_ref[...] = jnp.zeros(l_ref.shape, _F32)

    for row0, width in bands:
        scores(n_full, diag_start, row0, band, width, True)

    def full_chunks(chunk_fn):
        def two(i, carry):
            chunk_fn(2 * i)
            chunk_fn(2 * i + 1)
            return carry
        lax.fori_loop(0, n_full // 2, two, 0)

        @pl.when(n_full % 2 == 1)
        def _():
            chunk_fn(n_full - 1)

    full_chunks(lambda c: scores(c, pl.multiple_of(c * tk, tk), 0, tq, tk, False))

    for _, j in heads:
        row_max = jnp.max(m_ref[j], axis=-1, keepdims=True)
        m_ref[j] = jnp.broadcast_to(row_max, (tq, LANES))

    def weights(c, k_start, row0, rows, width, diag):
        for g, j in heads:
            groups = []
            for l0, gw in lane_groups(width):
                p = jnp.exp2(s_ref[j, c, row0:row0 + rows, l0:l0 + gw] - m_ref[j, row0:row0 + rows, :gw])
                groups.append((gw, p))
            fold(l_ref, j, row0, rows, groups, jnp.add, diag)
            pieces = [p.astype(_BF16) for _, p in groups]
            p_all = pieces[0] if len(pieces) == 1 else jnp.concatenate(pieces, axis=1)
            pv = jnp.dot(p_all, vv_ref[g, pl.ds(k_start, width), :], preferred_element_type=_F32)
            if diag:
                acc_ref[j, row0:row0 + rows, :] = pv
            else:
                acc_ref[j, row0:row0 + rows, :] += pv

    for row0, width in bands:
        weights(n_full, diag_start, row0, band, width, True)

    full_chunks(lambda c: weights(c, pl.multiple_of(c * tk, tk), 0, tq, tk, False))

    out_lane = lax.broadcasted_iota(jnp.int32, (tq, LANES), 1)
    for g in range(pairs):
        out = jnp.zeros((tq, LANES), _F32)
        for j in range(HEADS_PER_BLOCK):
            hh = g * HEADS_PER_BLOCK + j
            o = acc_ref[hh] / jnp.sum(l_ref[hh], axis=-1, keepdims=True)
            out = jnp.where(out_lane // HEAD_DIM == j, o, out)
        cols = slice(g * LANES, (g + 1) * LANES)
        a_ref[0, :, cols] = (out * sz_ref[0, :, cols].astype(_F32)).astype(_BF16)


def _attention(q, augq, sz, k_all, v_all, augk, *, tq, tk, n_heads, pairs):
    b, sq, a = q.shape
    skv = k_all.shape[1]
    past = skv - sq
    n_q = sq // tq
    width = pairs * LANES
    assert sq % tq == 0 and past % tk == 0 and (tq % tk == 0 or n_q == 1) and a == n_heads * HEAD_DIM
    assert a % width == 0
    n_chunks = (past + (n_q - 1) * tq) // tk + 1
    n_local = pairs * HEADS_PER_BLOCK
    return pl.pallas_call(
        functools.partial(_attention_kernel, past=past, tk=tk, n_heads=n_heads, pairs=pairs),
        grid=(b, a // width, n_q),
        in_specs=[
            pl.BlockSpec((1, tq, width), lambda i, p, t: (i, t, p)),
            pl.BlockSpec((1, tq, LANES), lambda i, p, t: (i, t, 0)),
            pl.BlockSpec((1, tq, width), lambda i, p, t: (i, t, p)),
            pl.BlockSpec((1, skv, width), lambda i, p, t: (i, 0, p)),
            pl.BlockSpec((1, skv, width), lambda i, p, t: (i, 0, p)),
            pl.BlockSpec((1, skv, LANES), lambda i, p, t: (i, 0, 0)),
        ],
        out_specs=pl.BlockSpec((1, tq, width), lambda i, p, t: (i, t, p)),
        out_shape=jax.ShapeDtypeStruct((b, sq, a), _BF16),
        scratch_shapes=[
            pltpu.VMEM((n_local, skv, 2 * LANES), _BF16),
            pltpu.VMEM((pairs, skv, LANES), _BF16),
            pltpu.VMEM((n_local, n_chunks, tq, max(tk, tq)), _F32),
            pltpu.VMEM((n_local, tq, LANES), _F32),
            pltpu.VMEM((n_local, tq, LANES), _F32),
            pltpu.VMEM((n_local, tq, LANES), _F32),
        ],
        compiler_params=_params("arbitrary", "arbitrary", "arbitrary"),
        name="attention",
    )(q, augq, sz, k_all, v_all, augk)


def _out_kernel(a_ref, x_ref, mod_ref, w_ref, g_ref, y_ref):
    o = jnp.dot(a_ref[0], w_ref[...], preferred_element_type=_F32)
    x = x_ref[0] + mod_ref[0, 2:3, :] * o
    ms = jnp.mean(x * x, axis=-1, keepdims=True)
    y_ref[0] = x * lax.rsqrt(ms + RMS_EPS) * g_ref[...]


def _out_layer(a, x, mod, w_out, final_g, *, tm):
    b, s, d = x.shape
    tok = lambda width: pl.BlockSpec((1, tm, width), lambda i, t: (i, t, 0))
    return pl.pallas_call(
        _out_kernel,
        grid=(b, s // tm),
        in_specs=[
            tok(a.shape[2]),
            tok(d),
            pl.BlockSpec((1, 3, d), lambda i, t: (i, 0, 0)),
            _const_spec(w_out.shape),
            _const_spec((1, d)),
        ],
        out_specs=tok(d),
        out_shape=jax.ShapeDtypeStruct((b, s, d), _F32),
        compiler_params=_params("arbitrary", "arbitrary"),
        name="out_layer",
    )(a, x, mod, w_out, final_g)


def _stream(x, mod0, mod1, hist, past_k, past_v, past_logf, w, *, tm, tq, tk):
    b, s, d = x.shape
    n_heads = w["n_heads"]
    x1, new_hist = _conv_layer(x, mod0, w["g0"], hist, w["conv_w_in"], w["conv_k"], w["conv_w_out"], tm=tm)
    q, k, v, sz, logf, logfx = _attn_proj(x1, mod1, w["g1"], w["attn_w_qkvz"], w["attn_wf_x"], w["attn_bf_x"],
                                          tm=tm, n_heads=n_heads)
    if past_k is None:
        k_all, v_all, logfx_all = k, v, logfx
    else:
        k_all = jnp.concatenate([past_k, k], axis=1)
        v_all = jnp.concatenate([past_v, v], axis=1)
        logfx_all = jnp.concatenate([jnp.tile(past_logf, (1, 1, LANES // n_heads)), logfx], axis=1)
    augq, augk = _forget_scan(logfx_all, sq=s, n_heads=n_heads)
    a = _attention(q, augq, sz, k_all, v_all, augk, tq=tq, tk=tk, n_heads=n_heads, pairs=ATTN_PAIRS_PER_STEP)
    y = _out_layer(a, x1, mod1, w["attn_w_out"], w["final_g"], tm=min(s, 2 * tm))
    return y, new_hist, k, v, logf


def kernel(x_prompt, x_sample, c_prompt, c_sample, state_conv, cache_k, cache_v, cache_logf, norm_g, ada_w, ada_b,
           conv_w_in, conv_k, conv_w_out, attn_w_in, attn_b_f, attn_w_out, final_g):
    bp, sp, d = x_prompt.shape
    bs, ss, _ = x_sample.shape
    n_heads = attn_b_f.shape[1]
    a = n_heads * HEAD_DIM
    past = cache_k.shape[2]
    assert norm_g.shape[0] == 2 and state_conv.shape[0] == 1 and cache_k.shape[0] == 1

    mod = _modulation(jnp.concatenate([c_prompt, c_sample], axis=0), ada_w, ada_b)
    mod = mod.reshape(2, bp + bs, 3, d)

    w_attn = attn_w_in[0]
    q_scale = jnp.concatenate([jnp.full((a,), LOG2_E / math.sqrt(HEAD_DIM), _F32), jnp.ones((3 * a,), _F32)])
    w = {
        "n_heads": n_heads,
        "g0": norm_g[0:1],
        "g1": norm_g[1:2],
        "conv_w_in": conv_w_in[0].astype(_BF16),
        "conv_k": conv_k[0],
        "conv_w_out": conv_w_out[0].astype(_BF16),
        "attn_w_qkvz": (w_attn[:, :4 * a] * q_scale).astype(_BF16),
        "attn_wf_x": jnp.tile(w_attn[:, 4 * a:], (1, LANES // n_heads)).astype(_BF16),
        "attn_bf_x": jnp.tile(attn_b_f, (1, LANES // n_heads)),
        "attn_w_out": attn_w_out[0].astype(_BF16),
        "final_g": final_g.reshape(1, d),
    }

    zero_hist = jnp.zeros((bp, CONV_TAPS - 1, conv_k.shape[2]), _F32)
    yp, hist_p, kp, vp, lfp = _stream(x_prompt, mod[0, :bp], mod[1, :bp], zero_hist, None, None, None, w,
                                      tm=512, tq=512, tk=512)
    ys, hist_s, ks, vs, lfs = _stream(x_sample, mod[0, bp:], mod[1, bp:], state_conv[0],
                                      cache_k[0].reshape(bs, past, a), cache_v[0].reshape(bs, past, a),
                                      cache_logf[0], w, tm=ss, tq=ss, tk=past)

    heads = lambda t: t.reshape(1, t.shape[0], t.shape[1], n_heads, HEAD_DIM)
    return (yp, ys, hist_p[None], heads(kp), heads(vp), lfp[None],
            hist_s[None], heads(ks), heads(vs), lfs[None])
```
